```python
import jax, jax.numpy as jnp
from jax import lax
import numpy as np

D_MODEL = 2048
BATCH = 4
SEQ = 2048
DEPTH = 4
DEC_BATCH = 128
DEC_SEQ = 8
PAST_LEN = 8192
PAGE_SIZE = 128

N_MIXERS = 4
EPS = 1e-6
HG_HEADS = 16
HG_DK = D_MODEL // HG_HEADS
HG_DV = D_MODEL // HG_HEADS
GLA_HEADS = 4
GLA_DK = D_MODEL // 2 // GLA_HEADS
GLA_DV = D_MODEL // GLA_HEADS
GLA_GATE_RANK = 16
GLA_GATE_NORM = 16.0
GLA_IN = 2 * GLA_HEADS * GLA_DK + 2 * GLA_HEADS * GLA_DV + GLA_GATE_RANK
LA_CHUNK = 64
MB_HEADS = 16
MB_KV_HEADS = 4
MB_HEAD_DIM = D_MODEL // MB_HEADS
MB_IN = (MB_HEADS + 2 * MB_KV_HEADS) * MB_HEAD_DIM
MB_BLOCK = 256
MB_TOPK = 3
MB_QUERY_SWEEP = 8
MLA_HEADS = 16
MLA_Q_RANK = 768
MLA_KV_RANK = 512
MLA_NOPE = 128
MLA_ROPE = 64
MLA_V = 128
ROPE_THETA = 10000.0
ATTN_QUERY_SWEEP = 128
FF_DENSE = 5632
N_EXPERTS = 8
TOP_K = 2
FF_EXPERT = 2816

kernel_name = "hybrid_hgrn2_gla_moba_mla_decode_step"


def _n_mixer_layers(m):
    return (DEPTH - m + N_MIXERS - 1) // N_MIXERS


def _sweep(length, block):
    return block if length % block == 0 else length


def _split(x, sizes):
    return jnp.split(x, [int(s) for s in np.cumsum(sizes)[:-1]], axis=-1)


def rmsnorm(x, g):
    xf = x.astype(jnp.float32)
    y = xf * lax.rsqrt(jnp.mean(xf * xf, axis=-1, keepdims=True) + EPS)
    return (y * g.astype(jnp.float32)).astype(x.dtype)


def rope(x, pos):
    half = x.shape[-1] // 2
    inv = ROPE_THETA ** (-jnp.arange(half, dtype=jnp.float32) / half)
    ang = pos.astype(jnp.float32)[:, None] * inv[None, :]
    cos, sin = jnp.cos(ang)[:, None, :], jnp.sin(ang)[:, None, :]
    xf = x.astype(jnp.float32)
    x1, x2 = xf[..., :half], xf[..., half:]
    return jnp.concatenate([x1 * cos - x2 * sin, x1 * sin + x2 * cos], axis=-1).astype(x.dtype)


def gated_linear_attention(q, k, v, log_f, s0, scale):
    B, L, H, _ = q.shape
    dv = v.shape[-1]
    c = _sweep(L, LA_CHUNK)
    n = L // c

    def chunks(t):
        return t.astype(jnp.float32).reshape(B, n, c, H, t.shape[-1]).transpose(1, 0, 3, 2, 4)

    causal = jnp.tril(jnp.ones((c, c), bool))[:, :, None]

    def step(s, inp):
        qi, ki, vi, gi = inp
        b = jnp.cumsum(gi, axis=2)
        decay = jnp.exp(jnp.where(causal, b[:, :, :, None, :] - b[:, :, None, :, :], -jnp.inf))
        scores = jnp.einsum('bhtk,bhsk,bhtsk->bhts', qi, ki, decay)
        b_end = b[:, :, -1:, :]
        o = jnp.einsum('bhts,bhsv->bhtv', scores, vi) + jnp.einsum('bhtk,bhkv->bhtv', qi * jnp.exp(b), s)
        s = jnp.exp(b_end[:, :, 0, :, None]) * s + jnp.einsum('bhsk,bhsv->bhkv', ki * jnp.exp(b_end - b), vi)
        return s, o

    s_fin, o = lax.scan(step, s0.astype(jnp.float32), (chunks(q * scale), chunks(k), chunks(v), chunks(log_f)))
    return o.transpose(1, 0, 3, 2, 4).reshape(B, L, H, dv), s_fin


def hgrn2_mixer(h, s0, w_in, lb, gnorm, w_out):
    B, L, _ = h.shape
    q, f, i, g = jnp.split(h @ w_in, 4, axis=-1)
    q = jax.nn.silu(q).reshape(B, L, HG_HEADS, HG_DK)
    lbf = lb.astype(jnp.float32)
    log_forget = jnp.logaddexp(jnp.log(lbf), jnp.log1p(-lbf) + jax.nn.log_sigmoid(f.astype(jnp.float32)))
    log_forget = log_forget.reshape(B, L, HG_HEADS, HG_DK)
    k_in = -jnp.expm1(log_forget)
    o, s = gated_linear_attention(q, k_in, i.reshape(B, L, HG_HEADS, HG_DV), log_forget, s0, HG_DK ** -0.5)
    o = rmsnorm(o, gnorm) * jax.nn.silu(g.reshape(B, L, HG_HEADS, HG_DV).astype(jnp.float32))
    return o.reshape(B, L, HG_HEADS * HG_DV).astype(h.dtype) @ w_out, s


def gla_mixer(h, s0, w_in, w_gate_up, b_gate, gnorm, w_out):
    B, L, _ = h.shape
    q, k, v, g, gl = _split(h @ w_in, [GLA_HEADS * GLA_DK, GLA_HEADS * GLA_DK, GLA_HEADS * GLA_DV,
                                      GLA_HEADS * GLA_DV, GLA_GATE_RANK])
    log_alpha = jax.nn.log_sigmoid((gl @ w_gate_up + b_gate).astype(jnp.float32)) / GLA_GATE_NORM
    o, s = gated_linear_attention(q.reshape(B, L, GLA_HEADS, GLA_DK), k.reshape(B, L, GLA_HEADS, GLA_DK),
                                  v.reshape(B, L, GLA_HEADS, GLA_DV), log_alpha.reshape(B, L, GLA_HEADS, GLA_DK),
                                  s0, GLA_DK ** -0.5)
    o = rmsnorm(o, gnorm) * jax.nn.silu(g.reshape(B, L, GLA_HEADS, GLA_DV).astype(jnp.float32))
    return o.reshape(B, L, GLA_HEADS * GLA_DV).astype(h.dtype) @ w_out, s


def moba_project(h, w_in):
    B, L, _ = h.shape
    q, k, v = _split(h @ w_in, [MB_HEADS * MB_HEAD_DIM, MB_KV_HEADS * MB_HEAD_DIM, MB_KV_HEADS * MB_HEAD_DIM])
    return (q.reshape(B, L, MB_HEADS, MB_HEAD_DIM), k.reshape(B, L, MB_KV_HEADS, MB_HEAD_DIM),
            v.reshape(B, L, MB_KV_HEADS, MB_HEAD_DIM))


def moba_attend(q, q_pos, k, v):
    Lq, H, dh = q.shape
    T = k.shape[0]
    n_blk = -(-T // MB_BLOCK)
    pad = n_blk * MB_BLOCK - T
    kb = jnp.pad(k, ((0, pad), (0, 0), (0, 0))).reshape(n_blk, MB_BLOCK, MB_KV_HEADS, dh)
    vb = jnp.pad(v, ((0, pad), (0, 0), (0, 0))).reshape(n_blk, MB_BLOCK, MB_KV_HEADS, dh)
    kv_of_head = jnp.arange(H) // (H // MB_KV_HEADS)
    k_mean = jnp.mean(kb.astype(jnp.float32), axis=1)[:, kv_of_head]
    top = min(MB_TOPK, n_blk)
    blk_ids = jnp.arange(n_blk)
    offs = jnp.arange(MB_BLOCK)
    scale = dh ** -0.5
    qb = _sweep(Lq, MB_QUERY_SWEEP)

    def block(args):
        qc, qp = args
        own = qp // MB_BLOCK
        gate = jnp.einsum('qhd,nhd->qhn', qc.astype(jnp.float32), k_mean)
        gate = jnp.where(blk_ids[None, None, :] < own[:, None, None], gate, -jnp.inf)
        _, sel = lax.top_k(gate, top)
        sel_ok = sel < own[:, None, None]
        blocks = jnp.concatenate([sel, jnp.broadcast_to(own[:, None, None], (qb, H, 1))], axis=-1)
        ks = kb[blocks, :, kv_of_head[None, :, None], :]
        vs = vb[blocks, :, kv_of_head[None, :, None], :]
        s = jnp.einsum('qhd,qhjtd->qhjt', qc, ks).astype(jnp.float32) * scale
        key_pos = blocks[..., None] * MB_BLOCK + offs
        ok = jnp.concatenate([jnp.broadcast_to(sel_ok[..., None], (qb, H, top, MB_BLOCK)),
                              key_pos[:, :, top:, :] <= qp[:, None, None, None]], axis=2)
        s = jnp.where(ok, s, -jnp.inf)
        p = jax.nn.softmax(s.reshape(qb, H, -1), axis=-1).reshape(s.shape).astype(v.dtype)
        return jnp.einsum('qhjt,qhjtd->qhd', p, vs)

    nb = Lq // qb
    out = lax.map(block, (q.reshape(nb, qb, H, dh), q_pos.reshape(nb, qb)))
    return out.reshape(Lq, H, dh)


def mla_project(h, pos, w_in, q_norm, w_q_up, kv_norm, w_kv_up):
    B, L, _ = h.shape
    qa, ckv, kr = _split(h @ w_in, [MLA_Q_RANK, MLA_KV_RANK, MLA_ROPE])
    q = (rmsnorm(qa, q_norm) @ w_q_up).reshape(B, L, MLA_HEADS, MLA_NOPE + MLA_ROPE)
    q_nope, q_rope = q[..., :MLA_NOPE], rope(q[..., MLA_NOPE:], pos)
    c = rmsnorm(ckv, kv_norm)
    k_rope = rope(kr[:, :, None, :], pos)[:, :, 0, :]
    w_uk = w_kv_up.reshape(MLA_KV_RANK, MLA_HEADS, MLA_NOPE + MLA_V)[..., :MLA_NOPE]
    q_lat = jnp.einsum('blhn,chn->blhc', q_nope, w_uk)
    return q_lat, q_rope, c, k_rope


def mla_attend(q_lat, q_rope, q_pos, c, kr):
    Lq, H, C = q_lat.shape
    R = q_rope.shape[-1]
    k_pos = jnp.arange(c.shape[0])
    scale = (MLA_NOPE + MLA_ROPE) ** -0.5
    qb = _sweep(Lq, ATTN_QUERY_SWEEP)

    def block(args):
        ql, qr, qp = args
        s = (jnp.einsum('qhc,tc->hqt', ql, c) + jnp.einsum('qhr,tr->hqt', qr, kr)).astype(jnp.float32) * scale
        s = jnp.where(k_pos[None, None, :] <= qp[None, :, None], s, -jnp.inf)
        p = jax.nn.softmax(s, axis=-1).astype(c.dtype)
        return jnp.einsum('hqt,tc->qhc', p, c)

    nb = Lq // qb
    out = lax.map(block, (q_lat.reshape(nb, qb, H, C), q_rope.reshape(nb, qb, H, R), q_pos.reshape(nb, qb)))
    return out.reshape(Lq, H, C)


def mla_output(o_lat, w_kv_up, w_out):
    B, L = o_lat.shape[:2]
    w_uv = w_kv_up.reshape(MLA_KV_RANK, MLA_HEADS, MLA_NOPE + MLA_V)[..., MLA_NOPE:]
    o = jnp.einsum('blhc,chv->blhv', o_lat, w_uv)
    return o.reshape(B, L, MLA_HEADS * MLA_V) @ w_out


def _gather_pages(cache, layer, pages):
    rows = cache[layer, pages]
    return rows.reshape((-1,) + rows.shape[2:])


def swiglu(h, w_in, w_out):
    g, u = jnp.split(h @ w_in, 2, axis=-1)
    return (jax.nn.silu(g) * u) @ w_out


def moe_swiglu(h, router, w_in, w_out):
    logits = (h @ router).astype(jnp.float32)
    top_val, top_idx = lax.top_k(logits, TOP_K)
    gates = jax.nn.softmax(top_val, axis=-1)
    weight = jnp.sum(jax.nn.one_hot(top_idx, N_EXPERTS, dtype=jnp.float32) * gates[..., None], axis=-2)
    y = jnp.zeros_like(h)
    for e in range(N_EXPERTS):
        y = y + weight[:, e:e + 1].astype(h.dtype) * swiglu(h, w_in[e], w_out[e])
    return y


def setup_inputs(seed: int = 0) -> dict:
    keys = iter(jax.random.split(jax.random.key(seed), 40))
    n_hg, n_gla, n_mb, n_mla = (_n_mixer_layers(m) for m in range(N_MIXERS))
    n_dense, n_moe = (DEPTH + 1) // 2, DEPTH // 2
    n_pages = PAST_LEN // PAGE_SIZE
    n_pool = (DEC_BATCH * n_pages * 5) // 4

    def normal(shape, scale=1.0):
        return scale * jax.random.normal(next(keys), shape, jnp.float32)

    def linear(shape):
        return normal(shape, shape[-2] ** -0.5)

    def gain(shape):
        return 1.0 + normal(shape, 0.02)

    page_table = jax.random.permutation(next(keys), n_pool)[: DEC_BATCH * n_pages]
    page_table = page_table.reshape(DEC_BATCH, n_pages).astype(jnp.int32)
    return {
        "x_prompt": normal((BATCH, SEQ, D_MODEL)),
        "x_sample": normal((DEC_BATCH, DEC_SEQ, D_MODEL)),
        "state_hgrn": normal((n_hg, DEC_BATCH, HG_HEADS, HG_DK, HG_DV), 0.5),
        "state_gla": normal((n_gla, DEC_BATCH, GLA_HEADS, GLA_DK, GLA_DV), 0.5),
        "cache_moba_k": normal((n_mb, n_pool, PAGE_SIZE, MB_KV_HEADS, MB_HEAD_DIM)),
        "cache_moba_v": normal((n_mb, n_pool, PAGE_SIZE, MB_KV_HEADS, MB_HEAD_DIM)),
        "cache_mla_latent": normal((n_mla, n_pool, PAGE_SIZE, MLA_KV_RANK)),
        "cache_mla_krope": normal((n_mla, n_pool, PAGE_SIZE, MLA_ROPE)),
        "page_table": page_table,
        "norm_mixer": gain((DEPTH, D_MODEL)),
        "norm_ffn": gain((DEPTH, D_MODEL)),
        "norm_final": gain((D_MODEL,)),
        "hgrn_w_in": linear((n_hg, D_MODEL, 4 * D_MODEL)),
        "hgrn_lb_logits": normal((DEPTH + 1, HG_HEADS * HG_DK), 0.1),
        "hgrn_gnorm": gain((n_hg, HG_DV)),
        "hgrn_w_out": linear((n_hg, HG_HEADS * HG_DV, D_MODEL)),
        "gla_w_in": linear((n_gla, D_MODEL, GLA_IN)),
        "gla_w_gate_up": linear((n_gla, GLA_GATE_RANK, GLA_HEADS * GLA_DK)),
        "gla_b_gate": normal((n_gla, GLA_HEADS * GLA_DK), 0.1),
        "gla_gnorm": gain((n_gla, GLA_DV)),
        "gla_w_out": linear((n_gla, GLA_HEADS * GLA_DV, D_MODEL)),
        "moba_w_in": linear((n_mb, D_MODEL, MB_IN)),
        "moba_w_out": linear((n_mb, MB_HEADS * MB_HEAD_DIM, D_MODEL)),
        "mla_w_in": linear((n_mla, D_MODEL, MLA_Q_RANK + MLA_KV_RANK + MLA_ROPE)),
        "mla_q_norm": gain((n_mla, MLA_Q_RANK)),
        "mla_w_q_up": linear((n_mla, MLA_Q_RANK, MLA_HEADS * (MLA_NOPE + MLA_ROPE))),
        "mla_kv_norm": gain((n_mla, MLA_KV_RANK)),
        "mla_w_kv_up": linear((n_mla, MLA_KV_RANK, MLA_HEADS * (MLA_NOPE + MLA_V))),
        "mla_w_out": linear((n_mla, MLA_HEADS * MLA_V, D_MODEL)),
        "ffn_w_in": linear((n_dense, D_MODEL, 2 * FF_DENSE)),
        "ffn_w_out": linear((n_dense, FF_DENSE, D_MODEL)),
        "moe_router": linear((n_moe, D_MODEL, N_EXPERTS)),
        "moe_w_in": linear((n_moe, N_EXPERTS, D_MODEL, 2 * FF_EXPERT)),
        "moe_w_out": linear((n_moe, N_EXPERTS, FF_EXPERT, D_MODEL)),
    }


def reference(x_prompt, x_sample, state_hgrn, state_gla, cache_moba_k, cache_moba_v, cache_mla_latent,
              cache_mla_krope, page_table, norm_mixer, norm_ffn, norm_final, hgrn_w_in, hgrn_lb_logits,
              hgrn_gnorm, hgrn_w_out, gla_w_in, gla_w_gate_up, gla_b_gate, gla_gnorm, gla_w_out, moba_w_in,
              moba_w_out, mla_w_in, mla_q_norm, mla_w_q_up, mla_kv_norm, mla_w_kv_up, mla_w_out, ffn_w_in,
              ffn_w_out, moe_router, moe_w_in, moe_w_out):
    pos_p = jnp.arange(SEQ, dtype=jnp.int32)
    pos_s = PAST_LEN + jnp.arange(DEC_SEQ, dtype=jnp.int32)
    lb_all = jnp.cumsum(jax.nn.softmax(hgrn_lb_logits.astype(jnp.float32), axis=0), axis=0)
    hg_p, hg_s, gla_p, gla_s = [], [], [], []
    mbk_p, mbv_p, mbk_s, mbv_s = [], [], [], []
    mlc_p, mlr_p, mlc_s, mlr_s = [], [], [], []
    xp, xs = x_prompt, x_sample
    for i in range(DEPTH):
        m, j = i % N_MIXERS, i // N_MIXERS
        hp, hs = rmsnorm(xp, norm_mixer[i]), rmsnorm(xs, norm_mixer[i])
        if m == 0:
            s0 = jnp.zeros((BATCH, HG_HEADS, HG_DK, HG_DV), jnp.float32)
            op, sp = hgrn2_mixer(hp, s0, hgrn_w_in[j], lb_all[i], hgrn_gnorm[j], hgrn_w_out[j])
            os_, ss = hgrn2_mixer(hs, state_hgrn[j], hgrn_w_in[j], lb_all[i], hgrn_gnorm[j], hgrn_w_out[j])
            hg_p.append(sp.astype(state_hgrn.dtype)); hg_s.append(ss.astype(state_hgrn.dtype))
        elif m == 1:
            s0 = jnp.zeros((BATCH, GLA_HEADS, GLA_DK, GLA_DV), jnp.float32)
            op, sp = gla_mixer(hp, s0, gla_w_in[j], gla_w_gate_up[j], gla_b_gate[j], gla_gnorm[j], gla_w_out[j])
            os_, ss = gla_mixer(hs, state_gla[j], gla_w_in[j], gla_w_gate_up[j], gla_b_gate[j], gla_gnorm[j],
                                gla_w_out[j])
            gla_p.append(sp.astype(state_gla.dtype)); gla_s.append(ss.astype(state_gla.dtype))
        elif m == 2:
            qp_, kp_, vp_ = moba_project(hp, moba_w_in[j])
            ap = jax.vmap(moba_attend, in_axes=(0, None, 0, 0))(qp_, pos_p, kp_, vp_)
            qs_, ks_, vs_ = moba_project(hs, moba_w_in[j])

            def moba_seq(args, j=j):
                q1, k1, v1, pages = args
                k_all = jnp.concatenate([_gather_pages(cache_moba_k, j, pages), k1], axis=0)
                v_all = jnp.concatenate([_gather_pages(cache_moba_v, j, pages), v1], axis=0)
                return moba_attend(q1, pos_s, k_all, v_all)

            as_ = lax.map(moba_seq, (qs_, ks_, vs_, page_table))
            op = ap.reshape(BATCH, SEQ, MB_HEADS * MB_HEAD_DIM) @ moba_w_out[j]
            os_ = as_.reshape(DEC_BATCH, DEC_SEQ, MB_HEADS * MB_HEAD_DIM) @ moba_w_out[j]
            mbk_p.append(kp_); mbv_p.append(vp_); mbk_s.append(ks_); mbv_s.append(vs_)
        else:
            qlp, qrp, cp, krp = mla_project(hp, pos_p, mla_w_in[j], mla_q_norm[j], mla_w_q_up[j],
                                            mla_kv_norm[j], mla_w_kv_up[j])
            olp = jax.vmap(mla_attend, in_axes=(0, 0, None, 0, 0))(qlp, qrp, pos_p, cp, krp)
            qls, qrs, cs, krs = mla_project(hs, pos_s, mla_w_in[j], mla_q_norm[j], mla_w_q_up[j],
                                            mla_kv_norm[j], mla_w_kv_up[j])

            def mla_seq(args, j=j):
                ql1, qr1, c1, kr1, pages = args
                c_all = jnp.concatenate([_gather_pages(cache_mla_latent, j, pages), c1], axis=0)
                kr_all = jnp.concatenate([_gather_pages(cache_mla_krope, j, pages), kr1], axis=0)
                return mla_attend(ql1, qr1, pos_s, c_all, kr_all)

            ols = lax.map(mla_seq, (qls, qrs, cs, krs, page_table))
            op = mla_output(olp, mla_w_kv_up[j], mla_w_out[j])
            os_ = mla_output(ols, mla_w_kv_up[j], mla_w_out[j])
            mlc_p.append(cp); mlr_p.append(krp); mlc_s.append(cs); mlr_s.append(krs)
        xp = xp + op.astype(xp.dtype)
        xs = xs + os_.astype(xs.dtype)
        ht = rmsnorm(jnp.concatenate([xp.reshape(-1, D_MODEL), xs.reshape(-1, D_MODEL)], axis=0), norm_ffn[i])
        if i % 2 == 0:
            yt = swiglu(ht, ffn_w_in[i // 2], ffn_w_out[i // 2])
        else:
            yt = moe_swiglu(ht, moe_router[i // 2], moe_w_in[i // 2], moe_w_out[i // 2])
        xp = xp + yt[: BATCH * SEQ].reshape(xp.shape).astype(xp.dtype)
        xs = xs + yt[BATCH * SEQ:].reshape(xs.shape).astype(xs.dtype)
    y_prompt = rmsnorm(xp, norm_final)
    y_sample = rmsnorm(xs, norm_final)
    return (y_prompt, y_sample,
            jnp.stack(hg_p), jnp.stack(hg_s), jnp.stack(gla_p), jnp.stack(gla_s),
            jnp.stack(mbk_p), jnp.stack(mbv_p), jnp.stack(mbk_s), jnp.stack(mbv_s),
            jnp.stack(mlc_p), jnp.stack(mlr_p), jnp.stack(mlc_s), jnp.stack(mlr_s))
```

```python
import functools

import jax
import jax.numpy as jnp
from jax import lax
from jax.experimental import pallas as pl
from jax.experimental.pallas import tpu as pltpu

F32 = jnp.float32
BF16 = jnp.bfloat16
NEG_INF = float("-inf")

D_MODEL = 2048
EPS = 1e-6
PAGE_SIZE = 128
HG_HEADS = 16
GLA_HEADS = 4
GLA_GATE_RANK = 16
GLA_GATE_NORM = 16.0
MB_HEADS = 16
MB_KV_HEADS = 4
MB_HEAD_DIM = 128
MB_BLOCK = 256
MB_TOPK = 3
MLA_HEADS = 16
MLA_Q_RANK = 768
MLA_KV_RANK = 512
MLA_NOPE = 128
MLA_ROPE = 64
MLA_V = 128
ROPE_THETA = 10000.0
N_EXPERTS = 8

LANES = 128
SUBLANES = 8
VMEM_LIMIT = 48 * 2 ** 20
LA_CHUNK = 64
PAGES_PER_STEP = 8
STATE_BLOCK_BYTES = 4 * 2 ** 20


def _cparams(*sem):
    return pltpu.CompilerParams(dimension_semantics=sem, vmem_limit_bytes=VMEM_LIMIT)


def _dot(a, b):
    return jnp.dot(a, b, preferred_element_type=F32)


def _dot_nt(a, b):
    return lax.dot_general(a, b, (((1,), (1,)), ((), ())), preferred_element_type=F32)


def _dot_tn(a, b):
    return lax.dot_general(a, b, (((0,), (0,)), ((), ())), preferred_element_type=F32)


def _norm_rows(x, g):
    return x * lax.rsqrt(jnp.mean(x * x, axis=-1, keepdims=True) + EPS) * g


def _lane_col(x, j):
    lane = lax.broadcasted_iota(jnp.int32, x.shape, 1)
    return jnp.sum(jnp.where(lane == j, x, 0.0), axis=-1, keepdims=True)


def _norm_matmul_kernel(x_ref, g_ref, w_ref, o_ref, h_ref):
    @pl.when(pl.program_id(1) == 0)
    def _():
        h_ref[...] = _norm_rows(x_ref[...], g_ref[...]).astype(BF16)

    o_ref[...] = _dot(h_ref[...], w_ref[...]).astype(o_ref.dtype)


def norm_matmul(x, g, w, *, tm, tn, out_dtype=F32):
    m = x.shape[0]
    k, n = w.shape
    return pl.pallas_call(
        _norm_matmul_kernel,
        grid=(m // tm, n // tn),
        in_specs=[pl.BlockSpec((tm, k), lambda i, j: (i, 0)),
                  pl.BlockSpec((1, k), lambda i, j: (0, 0)),
                  pl.BlockSpec((k, tn), lambda i, j: (0, j))],
        out_specs=pl.BlockSpec((tm, tn), lambda i, j: (i, j)),
        out_shape=jax.ShapeDtypeStruct((m, n), out_dtype),
        scratch_shapes=[pltpu.VMEM((tm, k), BF16)],
        compiler_params=_cparams("parallel", "arbitrary"),
        name="norm_matmul",
    )(x, g.reshape(1, k).astype(F32), w)


def _swiglu_kernel(*refs, has_norm, has_scale, nf):
    refs = list(refs)
    x_ref = refs.pop(0)
    g_ref = refs.pop(0) if has_norm else None
    wg_ref, wu_ref = refs.pop(0), refs.pop(0)
    rw_ref = refs.pop(0) if has_scale else None
    o_ref = refs.pop(0)
    if has_norm:
        h_ref = refs.pop(0)

        @pl.when(pl.program_id(1) == 0)
        def _():
            h_ref[...] = _norm_rows(x_ref[...], g_ref[...]).astype(BF16)

        h = h_ref[...]
    else:
        h = x_ref[...]
    a = _dot(h, wg_ref[...])
    u = _dot(h, wu_ref[...])
    act = a * jax.nn.sigmoid(a) * u
    if has_scale:
        act = act * _lane_col(rw_ref[...], pl.program_id(1) // nf)
    o_ref[...] = act.astype(o_ref.dtype)


def swiglu_proj(x, g, w, rw, *, tm, tn):
    m, k = x.shape
    e, _, f2 = w.shape
    f = f2 // 2
    nf = f // tn
    has_norm, has_scale = g is not None, rw is not None
    in_specs = [pl.BlockSpec((tm, k), lambda i, j: (i, 0))]
    args = [x]
    if has_norm:
        in_specs.append(pl.BlockSpec((1, k), lambda i, j: (0, 0)))
        args.append(g.reshape(1, k).astype(F32))
    in_specs += [pl.BlockSpec((None, k, tn), lambda i, j: (j // nf, 0, j % nf)),
                 pl.BlockSpec((None, k, tn), lambda i, j: (j // nf, 0, nf + j % nf))]
    args += [w, w]
    if has_scale:
        in_specs.append(pl.BlockSpec((tm, LANES), lambda i, j: (i, 0)))
        args.append(rw)
    return pl.pallas_call(
        functools.partial(_swiglu_kernel, has_norm=has_norm, has_scale=has_scale, nf=nf),
        grid=(m // tm, e * nf),
        in_specs=in_specs,
        out_specs=pl.BlockSpec((tm, tn), lambda i, j: (i, j)),
        out_shape=jax.ShapeDtypeStruct((m, e * f), BF16),
        scratch_shapes=[pltpu.VMEM((tm, k), BF16)] if has_norm else [],
        compiler_params=_cparams("parallel", "arbitrary"),
        name="swiglu_proj",
    )(*args)


def _matmul_res_kernel(a_ref, w_ref, r_ref, o_ref):
    @pl.when(pl.program_id(2) == 0)
    def _():
        o_ref[...] = r_ref[...]

    o_ref[...] += _dot(a_ref[...], w_ref[...])


def matmul_res(a, w, res, *, tm, tn, tk):
    m, k = a.shape
    n = w.shape[1]
    return pl.pallas_call(
        _matmul_res_kernel,
        grid=(m // tm, n // tn, k // tk),
        in_specs=[pl.BlockSpec((tm, tk), lambda i, j, kk: (i, kk)),
                  pl.BlockSpec((tk, tn), lambda i, j, kk: (kk, j)),
                  pl.BlockSpec((tm, tn), lambda i, j, kk: (i, j))],
        out_specs=pl.BlockSpec((tm, tn), lambda i, j, kk: (i, j)),
        out_shape=jax.ShapeDtypeStruct((m, n), F32),
        compiler_params=_cparams("parallel", "parallel", "arbitrary"),
        name="matmul_res",
    )(a, w, res)


def _rmsnorm_kernel(x_ref, g_ref, o_ref):
    o_ref[...] = _norm_rows(x_ref[...], g_ref[...]).astype(o_ref.dtype)


def rmsnorm_rows(x, g, *, col=0, tm, out_dtype=F32):
    m = x.shape[0]
    k = g.shape[-1]
    return pl.pallas_call(
        _rmsnorm_kernel,
        grid=(m // tm,),
        in_specs=[pl.BlockSpec((tm, k), lambda i: (i, col)),
                  pl.BlockSpec((1, k), lambda i: (0, 0))],
        out_specs=pl.BlockSpec((tm, k), lambda i: (i, 0)),
        out_shape=jax.ShapeDtypeStruct((m, k), out_dtype),
        compiler_params=_cparams("parallel"),
        name="rmsnorm_rows",
    )(x, g.reshape(1, k).astype(F32))


def _router_kernel(x_ref, g_ref, wr_ref, rw_ref, h_ref, *, n_experts):
    h = _norm_rows(x_ref[...], g_ref[...])
    h_ref[...] = h.astype(BF16)
    logits = jnp.dot(h, wr_ref[...], preferred_element_type=F32, precision=lax.Precision.HIGHEST)
    lane = lax.broadcasted_iota(jnp.int32, logits.shape, 1)
    l1 = jnp.where(lane < n_experts, logits, NEG_INF)
    m1 = jnp.max(l1, axis=-1, keepdims=True)
    i1 = jnp.min(jnp.where(l1 == m1, lane, LANES), axis=-1, keepdims=True)
    l2 = jnp.where(lane == i1, NEG_INF, l1)
    m2 = jnp.max(l2, axis=-1, keepdims=True)
    i2 = jnp.min(jnp.where(l2 == m2, lane, LANES), axis=-1, keepdims=True)
    e2 = jnp.exp(m2 - m1)
    den = 1.0 + e2
    rw_ref[...] = jnp.where(lane == i1, 1.0 / den, 0.0) + jnp.where(lane == i2, e2 / den, 0.0)


def router(x, g, wr, *, tm):
    m, k = x.shape
    n_experts = wr.shape[1]
    wr_pad = jnp.pad(wr.astype(F32), ((0, 0), (0, LANES - n_experts)))
    return pl.pallas_call(
        functools.partial(_router_kernel, n_experts=n_experts),
        grid=(m // tm,),
        in_specs=[pl.BlockSpec((tm, k), lambda i: (i, 0)),
                  pl.BlockSpec((1, k), lambda i: (0, 0)),
                  pl.BlockSpec((k, LANES), lambda i: (0, 0))],
        out_specs=[pl.BlockSpec((tm, LANES), lambda i: (i, 0)),
                   pl.BlockSpec((tm, k), lambda i: (i, 0))],
        out_shape=[jax.ShapeDtypeStruct((m, LANES), F32), jax.ShapeDtypeStruct((m, k), BF16)],
        compiler_params=_cparams("parallel"),
        name="router",
    )(x, g.reshape(1, k).astype(F32), wr_pad)


def _head_matmul_kernel(a_ref, w_ref, o_ref):
    o_ref[...] = _dot(a_ref[...].astype(BF16), w_ref[...]).astype(o_ref.dtype)


def head_matmul(a, w, *, tm, out_dtype=BF16):
    m = a.shape[0]
    h, k, n = w.shape
    return pl.pallas_call(
        _head_matmul_kernel,
        grid=(m // tm, h),
        in_specs=[pl.BlockSpec((tm, k), lambda i, j: (i, j)),
                  pl.BlockSpec((None, k, n), lambda i, j: (j, 0, 0))],
        out_specs=pl.BlockSpec((tm, n), lambda i, j: (i, j)),
        out_shape=jax.ShapeDtypeStruct((m, h * n), out_dtype),
        compiler_params=_cparams("parallel", "arbitrary"),
        name="head_matmul",
    )(a, w)


def _rope_kernel(x1_ref, x2_ref, cos_ref, sin_ref, r1_ref, r2_ref, *, reps):
    cos, sin = cos_ref[...], sin_ref[...]
    if reps > 1:
        cos = jnp.concatenate([cos] * reps, axis=-1)
        sin = jnp.concatenate([sin] * reps, axis=-1)
    x1, x2 = x1_ref[...], x2_ref[...]
    r1_ref[...] = (x1 * cos - x2 * sin).astype(r1_ref.dtype)
    r2_ref[...] = (x1 * sin + x2 * cos).astype(r2_ref.dtype)


def rope_pairs(x, cos, sin, *, col1, col2, width, tm, out_dtype):
    m = x.shape[0]
    return pl.pallas_call(
        functools.partial(_rope_kernel, reps=width // LANES),
        grid=(m // tm,),
        in_specs=[pl.BlockSpec((tm, width), lambda i: (i, col1)),
                  pl.BlockSpec((tm, width), lambda i: (i, col2)),
                  pl.BlockSpec((tm, LANES), lambda i: (i, 0)),
                  pl.BlockSpec((tm, LANES), lambda i: (i, 0))],
        out_specs=[pl.BlockSpec((tm, width), lambda i: (i, 0))] * 2,
        out_shape=[jax.ShapeDtypeStruct((m, width), out_dtype)] * 2,
        compiler_params=_cparams("parallel"),
        name="rope_pairs",
    )(x, x, cos, sin)


def _cumsum_rows(x, c):
    row = lax.broadcasted_iota(jnp.int32, x.shape, 0)
    sh = 1
    while sh < c:
        x = x + jnp.where(row >= sh, pltpu.roll(x, sh, 0), 0.0)
        sh *= 2
    return x


def _gla_chunk(q, k, v, lf, s_prev, c):
    dk = q.shape[1]
    b = _cumsum_rows(lf, c)
    rid = lax.broadcasted_iota(jnp.int32, (c, 1), 0) % SUBLANES
    o = jnp.sum(q * k, axis=-1, keepdims=True) * v
    for d in range(1, SUBLANES):
        kd, bd, vd = pltpu.roll(k, d, 0), pltpu.roll(b, d, 0), pltpu.roll(v, d, 0)
        w = jnp.sum(q * kd * jnp.exp(jnp.minimum(b - bd, 0.0)), axis=-1, keepdims=True)
        o = o + jnp.where(rid >= d, w, 0.0) * vd
    if c > SUBLANES:
        t = lax.broadcasted_iota(jnp.int32, (c, c), 0)
        s = lax.broadcasted_iota(jnp.int32, (c, c), 1)
        a = jnp.zeros((c, c), F32)
        blk = 2 * SUBLANES
        while blk <= c:
            half = blk // 2
            ref = jnp.concatenate(
                [jnp.broadcast_to(b[i * blk + half - 1:i * blk + half, :], (blk, dk)) for i in range(c // blk)], axis=0)
            qt = (q * jnp.exp(jnp.minimum(b - ref, 0.0))).astype(BF16)
            kt = (k * jnp.exp(jnp.minimum(ref - b, 0.0))).astype(BF16)
            pair = (t // blk == s // blk) & (t % blk >= half) & (s % blk < half)
            a = a + jnp.where(pair, _dot_nt(qt, kt), 0.0)
            blk *= 2
        o = o + _dot(a.astype(BF16), v.astype(BF16))
    o = o + _dot((q * jnp.exp(b)).astype(BF16), s_prev.astype(BF16))
    b_end = b[c - 1:c, :]
    kt = (k * jnp.exp(b_end - b)).astype(BF16)
    decay = jnp.exp(jnp.sum(lf.T, axis=-1, keepdims=True))
    s_new = decay * s_prev + _dot_tn(kt, v.astype(BF16))
    return o, s_new


def _lin_attn_kernel(*refs, mode, sample, c, n_inner, scale):
    refs = list(refs)
    if mode == "hgrn":
        q_ref, f_ref, v_ref, g_ref, lb_ref, gn_ref = refs[:6]
        refs = refs[6:]
    else:
        q_ref, k_ref, v_ref, g_ref, gl_ref, wg_ref, bg_ref, gn_ref = refs[:8]
        refs = refs[8:]
    if sample:
        s0_ref, o_ref, so_ref = refs
    else:
        o_ref, so_ref, s_scr = refs

    def prep(rows):
        if mode == "hgrn":
            qr, fr, lb = q_ref[rows, :], f_ref[rows, :], lb_ref[...]
            q = qr * jax.nn.sigmoid(qr) * scale
            lf = jnp.log(lb + (1.0 - lb) * jax.nn.sigmoid(fr))
            k = (1.0 - lb) * jax.nn.sigmoid(-fr)
        else:
            q = q_ref[rows, :] * scale
            k = k_ref[rows, :]
            z = _dot(gl_ref[rows, :].astype(BF16), wg_ref[...]) + bg_ref[...]
            lf = -(jnp.maximum(-z, 0.0) + jnp.log1p(jnp.exp(-jnp.abs(z)))) * (1.0 / GLA_GATE_NORM)
        return q, k, v_ref[rows, :], lf

    def finish(rows, o):
        g = g_ref[rows, :]
        o_ref[rows, :] = (_norm_rows(o, gn_ref[...]) * (g * jax.nn.sigmoid(g))).astype(o_ref.dtype)

    if sample:
        def body(si, carry):
            rows = pl.ds(pl.multiple_of(si * c, c), c)
            q, k, v, lf = prep(rows)
            o, s_new = _gla_chunk(q, k, v, lf, s0_ref[si], c)
            so_ref[si] = s_new
            finish(rows, o)
            return carry
    else:
        @pl.when(pl.program_id(2) == 0)
        def _():
            s_scr[...] = jnp.zeros_like(s_scr)

        def body(ci, carry):
            rows = pl.ds(pl.multiple_of(ci * c, c), c)
            q, k, v, lf = prep(rows)
            o, s_new = _gla_chunk(q, k, v, lf, s_scr[...], c)
            s_scr[...] = s_new
            finish(rows, o)
            return carry

    lax.fori_loop(0, n_inner, body, 0)

    if not sample:
        @pl.when(pl.program_id(2) == pl.num_programs(2) - 1)
        def _():
            so_ref[...] = s_scr[...]


def lin_attn(p, extra, gnorm, s0, *, mode, heads, dk, dv, n_seq, seq_len, row0, scale):
    sample = s0 is not None
    if sample:
        c = seq_len
        nb = max(1, min(n_seq, STATE_BLOCK_BYTES // (dk * dv * 4)))
        tb, n_inner = nb * c, nb
        grid = (n_seq // nb, heads, 1)
        rb = row0 // tb
        row = lambda b, h, i: rb + b
    else:
        c, tb = LA_CHUNK, 4 * LA_CHUNK
        n_inner = tb // c
        nt = seq_len // tb
        grid = (n_seq, heads, nt)
        rb = row0 // tb
        row = lambda b, h, i: rb + b * nt + i
    if mode == "hgrn":
        lb, = extra
        nh = heads
        in_specs = [pl.BlockSpec((tb, dk), lambda b, h, i: (row(b, h, i), h)),
                    pl.BlockSpec((tb, dk), lambda b, h, i: (row(b, h, i), nh + h)),
                    pl.BlockSpec((tb, dv), lambda b, h, i: (row(b, h, i), 2 * nh + h)),
                    pl.BlockSpec((tb, dv), lambda b, h, i: (row(b, h, i), 3 * nh + h)),
                    pl.BlockSpec((1, dk), lambda b, h, i: (0, h)),
                    pl.BlockSpec((1, dv), lambda b, h, i: (0, 0))]
        args = [p, p, p, p, lb.reshape(1, -1).astype(F32), gnorm.reshape(1, dv).astype(F32)]
    else:
        wg, bg = extra
        nh = heads
        voff = 2 * heads * dk // dv
        gloff = (2 * heads * dk + 2 * heads * dv) // LANES
        in_specs = [pl.BlockSpec((tb, dk), lambda b, h, i: (row(b, h, i), h)),
                    pl.BlockSpec((tb, dk), lambda b, h, i: (row(b, h, i), nh + h)),
                    pl.BlockSpec((tb, dv), lambda b, h, i: (row(b, h, i), voff + h)),
                    pl.BlockSpec((tb, dv), lambda b, h, i: (row(b, h, i), voff + nh + h)),
                    pl.BlockSpec((tb, LANES), lambda b, h, i: (row(b, h, i), gloff)),
                    pl.BlockSpec((LANES, dk), lambda b, h, i: (0, h)),
                    pl.BlockSpec((1, dk), lambda b, h, i: (0, h)),
                    pl.BlockSpec((1, dv), lambda b, h, i: (0, 0))]
        args = [p, p, p, p, p, wg, bg.reshape(1, -1).astype(F32), gnorm.reshape(1, dv).astype(F32)]
    out_specs = [pl.BlockSpec((tb, dv), lambda b, h, i: (row(b, h, i) - rb, h))]
    out_shape = [jax.ShapeDtypeStruct((n_seq * seq_len, heads * dv), BF16)]
    scratch = []
    if sample:
        in_specs.append(pl.BlockSpec((nb, None, dk, dv), lambda b, h, i: (b, h, 0, 0)))
        args.append(s0)
        out_specs.append(pl.BlockSpec((nb, None, dk, dv), lambda b, h, i: (b, h, 0, 0)))
    else:
        out_specs.append(pl.BlockSpec((None, None, dk, dv), lambda b, h, i: (b, h, 0, 0)))
        scratch.append(pltpu.VMEM((dk, dv), F32))
    out_shape.append(jax.ShapeDtypeStruct((n_seq, heads, dk, dv), F32))
    return pl.pallas_call(
        functools.partial(_lin_attn_kernel, mode=mode, sample=sample, c=c, n_inner=n_inner, scale=scale),
        grid=grid,
        in_specs=in_specs,
        out_specs=out_specs,
        out_shape=out_shape,
        scratch_shapes=scratch,
        compiler_params=_cparams("parallel", "parallel", "arbitrary"),
        name=f"lin_attn_{mode}_{'sample' if sample else 'prompt'}",
    )(*args)


def _block_partial(s_list, v_list):
    m = functools.reduce(jnp.maximum, [jnp.max(s, axis=-1, keepdims=True) for s in s_list])
    ps = [jnp.exp(s - m) for s in s_list]
    l = functools.reduce(jnp.add, [jnp.sum(p, axis=-1, keepdims=True) for p in ps])
    o = functools.reduce(jnp.add, [_dot(p.astype(BF16), v) for p, v in zip(ps, v_list)])
    return m, l, o


def _put_lane(ref, j, val):
    lane = lax.broadcasted_iota(jnp.int32, ref.shape, 1)
    ref[...] = jnp.where(lane == j, val, ref[...])


def _moba_select(gate, m_all, n_prev, own, n_static):
    lane = lax.broadcasted_iota(jnp.int32, gate.shape, 1)
    cnt = jnp.zeros(gate.shape, F32)
    for jp in range(n_static):
        col = gate[:, jp:jp + 1]
        beats = (col > gate) | ((col == gate) & (jp < lane))
        cnt = cnt + jnp.where(beats & (jp < n_prev), 1.0, 0.0)
    sel = ((cnt < MB_TOPK) & (lane < n_prev)) | (lane == own)
    mv = jnp.where(sel, m_all, NEG_INF)
    w = jnp.where(sel, jnp.exp(mv - jnp.max(mv, axis=-1, keepdims=True)), 0.0)
    return w


def _moba_prompt_kernel(q_ref, k_ref, v_ref, o_ref, m_scr, l_scr, g_scr, o_scr, *, scale, group, n_blk):
    i = pl.program_id(2)
    dh = k_ref.shape[1]
    tpos = lax.broadcasted_iota(jnp.int32, (MB_BLOCK, MB_BLOCK), 0)
    spos = lax.broadcasted_iota(jnp.int32, (MB_BLOCK, MB_BLOCK), 1)
    causal = spos <= tpos
    for hh in range(group):
        qh = (q_ref[:, hh * dh:(hh + 1) * dh] * scale).astype(BF16)
        m_scr[...] = jnp.full(m_scr.shape, NEG_INF, F32)
        g_scr[...] = jnp.full(g_scr.shape, NEG_INF, F32)
        l_scr[...] = jnp.zeros(l_scr.shape, F32)

        def blk(j, carry):
            rows = pl.ds(pl.multiple_of(j * MB_BLOCK, MB_BLOCK), MB_BLOCK)
            s = _dot_nt(qh, k_ref[rows, :].astype(BF16))
            _put_lane(g_scr, j, jnp.sum(s, axis=-1, keepdims=True))
            s = jnp.where(causal | (j < i), s, NEG_INF)
            m, l, o = _block_partial([s], [v_ref[rows, :].astype(BF16)])
            _put_lane(m_scr, j, m)
            _put_lane(l_scr, j, l)
            o_scr[j] = o
            return carry

        lax.fori_loop(0, i + 1, blk, 0)
        w = _moba_select(g_scr[...], m_scr[...], i, i, n_blk)
        den = jnp.sum(w * l_scr[...], axis=-1, keepdims=True)

        def comb(j, acc):
            return acc + _lane_col(w, j) * o_scr[j]

        acc = lax.fori_loop(0, i + 1, comb, jnp.zeros((MB_BLOCK, dh), F32))
        o_ref[:, hh * dh:(hh + 1) * dh] = (acc / den).astype(o_ref.dtype)


def moba_prompt(p, *, n_seq, seq_len, heads, kv_heads, dh):
    group = heads // kv_heads
    nq = seq_len // MB_BLOCK
    koff = heads
    voff = heads + kv_heads
    return pl.pallas_call(
        functools.partial(_moba_prompt_kernel, scale=dh ** -0.5, group=group, n_blk=nq),
        grid=(n_seq, kv_heads, nq),
        in_specs=[pl.BlockSpec((MB_BLOCK, group * dh), lambda b, g, i: (b * nq + i, g)),
                  pl.BlockSpec((seq_len, dh), lambda b, g, i: (b, koff + g)),
                  pl.BlockSpec((seq_len, dh), lambda b, g, i: (b, voff + g))],
        out_specs=pl.BlockSpec((MB_BLOCK, group * dh), lambda b, g, i: (b * nq + i, g)),
        out_shape=jax.ShapeDtypeStruct((n_seq * seq_len, heads * dh), BF16),
        scratch_shapes=[pltpu.VMEM((MB_BLOCK, LANES), F32)] * 3 + [pltpu.VMEM((nq, MB_BLOCK, dh), F32)],
        compiler_params=_cparams("parallel", "parallel", "arbitrary"),
        name="moba_prompt",
    )(p, p, p)


def _moba_sample_kernel(pt_ref, q_ref, kn_ref, vn_ref, *refs, scale, pages, kv_heads, n_new):
    del pt_ref
    k_refs, v_refs = refs[:pages], refs[pages:2 * pages]
    o_ref, m_scr, l_scr, g_scr, o_scr = refs[2 * pages:]
    g = pl.program_id(1)
    rows = q_ref.shape[0]
    rpk = rows // kv_heads
    q = (q_ref[...] * scale).astype(BF16)
    ncol = k_refs[0].shape[0]
    r = lax.broadcasted_iota(jnp.int32, (rows, ncol), 0)
    cidx = lax.broadcasted_iota(jnp.int32, (rows, ncol), 1)
    valid = (cidx % kv_heads) == (r // rpk)
    ppb = MB_BLOCK // PAGE_SIZE
    bps = pages // ppb

    @pl.when(g == 0)
    def _():
        m_scr[...] = jnp.full(m_scr.shape, NEG_INF, F32)
        g_scr[...] = jnp.full(g_scr.shape, NEG_INF, F32)
        l_scr[...] = jnp.zeros(l_scr.shape, F32)

    for bb in range(bps):
        j = g * bps + bb
        ss = [_dot_nt(q, k_refs[bb * ppb + t][...].astype(BF16)) for t in range(ppb)]
        gsum = functools.reduce(jnp.add, [jnp.sum(jnp.where(valid, s, 0.0), axis=-1, keepdims=True) for s in ss])
        ss = [jnp.where(valid, s, NEG_INF) for s in ss]
        m, l, o = _block_partial(ss, [v_refs[bb * ppb + t][...].astype(BF16) for t in range(ppb)])
        _put_lane(g_scr, j, gsum)
        _put_lane(m_scr, j, m)
        _put_lane(l_scr, j, l)
        o_scr[j] = o

    @pl.when(g == pl.num_programs(1) - 1)
    def _():
        n_prev = o_scr.shape[0] - 1
        nc = n_new * kv_heads
        rr = lax.broadcasted_iota(jnp.int32, (rows, nc), 0)
        cc = lax.broadcasted_iota(jnp.int32, (rows, nc), 1)
        ok = ((cc % kv_heads) == (rr // rpk)) & ((cc // kv_heads) <= (rr % n_new))
        s = jnp.where(ok, _dot_nt(q, kn_ref[...].astype(BF16)), NEG_INF)
        m, l, o = _block_partial([s], [vn_ref[...].astype(BF16)])
        _put_lane(m_scr, n_prev, m)
        _put_lane(l_scr, n_prev, l)
        o_scr[n_prev] = o
        w = _moba_select(g_scr[...], m_scr[...], n_prev, n_prev, n_prev)
        den = jnp.sum(w * l_scr[...], axis=-1, keepdims=True)
        acc = jnp.zeros(o_ref.shape, F32)
        for jj in range(n_prev + 1):
            acc = acc + w[:, jj:jj + 1] * o_scr[jj]
        o_ref[...] = (acc / den).astype(o_ref.dtype)


def moba_sample(q, k_new, v_new, cache_k, cache_v, page_table, layer, *, kv_heads, dh):
    n_seq, rows, _ = q.shape
    n_new = k_new.shape[1] // kv_heads
    n_pages = page_table.shape[1]
    pages = PAGES_PER_STEP
    n_prev = n_pages * PAGE_SIZE // MB_BLOCK
    assert n_pages % pages == 0 and pages % (MB_BLOCK // PAGE_SIZE) == 0 and n_prev + 1 <= LANES
    assert n_new <= MB_BLOCK

    def page_spec(t):
        return pl.BlockSpec((None, None, PAGE_SIZE * kv_heads, dh),
                            lambda b, g, pt: (layer, pt[b, g * pages + t], 0, 0))

    grid_spec = pltpu.PrefetchScalarGridSpec(
        num_scalar_prefetch=1,
        grid=(n_seq, n_pages // pages),
        in_specs=[pl.BlockSpec((None, rows, dh), lambda b, g, pt: (b, 0, 0)),
                  pl.BlockSpec((None, n_new * kv_heads, dh), lambda b, g, pt: (b, 0, 0)),
                  pl.BlockSpec((None, n_new * kv_heads, dh), lambda b, g, pt: (b, 0, 0))]
        + [page_spec(t) for t in range(pages)] * 2,
        out_specs=pl.BlockSpec((None, rows, dh), lambda b, g, pt: (b, 0, 0)),
        scratch_shapes=[pltpu.VMEM((rows, LANES), F32)] * 3 + [pltpu.VMEM((n_prev + 1, rows, dh), F32)],
    )
    return pl.pallas_call(
        functools.partial(_moba_sample_kernel, scale=dh ** -0.5, pages=pages, kv_heads=kv_heads, n_new=n_new),
        grid_spec=grid_spec,
        out_shape=jax.ShapeDtypeStruct((n_seq, rows, dh), F32),
        compiler_params=_cparams("parallel", "arbitrary"),
        name="moba_sample",
    )(page_table, q, k_new, v_new, *([cache_k] * pages), *([cache_v] * pages))


def _flash_update(s, v, m_scr, l_scr, acc_scr):
    m_old = m_scr[...]
    m_new = jnp.maximum(m_old, jnp.max(s, axis=-1, keepdims=True))
    alpha = jnp.exp(m_old - m_new)
    p = jnp.exp(s - m_new)
    l_scr[...] = alpha * l_scr[...] + jnp.sum(p, axis=-1, keepdims=True)
    acc_scr[...] = alpha * acc_scr[...] + _dot(p.astype(BF16), v)
    m_scr[...] = m_new


def _flash_init(m_scr, l_scr, acc_scr):
    m_scr[...] = jnp.full(m_scr.shape, NEG_INF, F32)
    l_scr[...] = jnp.zeros(l_scr.shape, F32)
    acc_scr[...] = jnp.zeros(acc_scr.shape, F32)


def _mla_prompt_kernel(ql_ref, qr_ref, c_ref, kr_ref, o_ref, m_scr, l_scr, acc_scr, *, scale, tq, tk, heads):
    i = pl.program_id(1)
    rows = tq * heads
    _flash_init(m_scr, l_scr, acc_scr)
    qpos = i * tq + lax.broadcasted_iota(jnp.int32, (rows, tk), 0) // heads
    koff = lax.broadcasted_iota(jnp.int32, (rows, tk), 1)

    def body(j, carry):
        r0 = pl.multiple_of(j * tk, tk)
        cj = c_ref[pl.ds(r0, tk), :]
        s = (_dot_nt(ql_ref[...], cj) + _dot_nt(qr_ref[...], kr_ref[pl.ds(r0, tk), :])) * scale
        s = jnp.where(r0 + koff <= qpos, s, NEG_INF)
        _flash_update(s, cj, m_scr, l_scr, acc_scr)
        return carry

    lax.fori_loop(0, ((i + 1) * tq + tk - 1) // tk, body, 0)
    o_ref[...] = (acc_scr[...] / l_scr[...]).astype(o_ref.dtype)


def mla_prompt(ql, qr, c, kr, *, n_seq, seq_len, heads, scale):
    tq, tk = 128, 256
    nq = seq_len // tq
    cdim, rdim = c.shape[1], kr.shape[1]
    rows = tq * heads
    return pl.pallas_call(
        functools.partial(_mla_prompt_kernel, scale=scale, tq=tq, tk=tk, heads=heads),
        grid=(n_seq, nq),
        in_specs=[pl.BlockSpec((rows, cdim), lambda b, i: (b * nq + i, 0)),
                  pl.BlockSpec((rows, rdim), lambda b, i: (b * nq + i, 0)),
                  pl.BlockSpec((seq_len, cdim), lambda b, i: (b, 0)),
                  pl.BlockSpec((seq_len, rdim), lambda b, i: (b, 0))],
        out_specs=pl.BlockSpec((rows, cdim), lambda b, i: (b * nq + i, 0)),
        out_shape=jax.ShapeDtypeStruct((n_seq * seq_len * heads, cdim), BF16),
        scratch_shapes=[pltpu.VMEM((rows, 1), F32), pltpu.VMEM((rows, 1), F32), pltpu.VMEM((rows, cdim), F32)],
        compiler_params=_cparams("parallel", "arbitrary"),
        name="mla_prompt",
    )(ql, qr, c, kr)


def _mla_sample_kernel(pt_ref, ql_ref, qr_ref, cn_ref, krn_ref, *refs, scale, pages, heads):
    del pt_ref
    c_refs, kr_refs = refs[:pages], refs[pages:2 * pages]
    o_ref, m_scr, l_scr, acc_scr = refs[2 * pages:]
    g = pl.program_id(1)

    @pl.when(g == 0)
    def _():
        _flash_init(m_scr, l_scr, acc_scr)

    ql, qr = ql_ref[...], qr_ref[...]
    for t in range(pages):
        cj = c_refs[t][...].astype(BF16)
        s = (_dot_nt(ql, cj) + _dot_nt(qr, kr_refs[t][...].astype(BF16))) * scale
        _flash_update(s, cj, m_scr, l_scr, acc_scr)

    @pl.when(g == pl.num_programs(1) - 1)
    def _():
        cj = cn_ref[...].astype(BF16)
        n_new = cj.shape[0]
        s = (_dot_nt(ql, cj) + _dot_nt(qr, krn_ref[...].astype(BF16))) * scale
        rr = lax.broadcasted_iota(jnp.int32, (ql.shape[0], n_new), 0)
        cc = lax.broadcasted_iota(jnp.int32, (ql.shape[0], n_new), 1)
        s = jnp.where(cc <= rr // heads, s, NEG_INF)
        _flash_update(s, cj, m_scr, l_scr, acc_scr)
        o_ref[...] = (acc_scr[...] / l_scr[...]).astype(o_ref.dtype)


def mla_sample(ql, qr, c_new, kr_new, cache_c, cache_kr, page_table, layer, *, row0, heads, scale):
    n_seq, n_pages = page_table.shape
    pages = PAGES_PER_STEP
    n_new = c_new.shape[0] // n_seq
    rows = n_new * heads
    cdim, rdim = c_new.shape[1], kr_new.shape[1]
    rb = row0 * heads // rows
    assert n_pages % pages == 0

    def c_spec(t):
        return pl.BlockSpec((None, None, PAGE_SIZE, cdim), lambda b, g, pt: (layer, pt[b, g * pages + t], 0, 0))

    def kr_spec(t):
        return pl.BlockSpec((None, None, PAGE_SIZE, rdim), lambda b, g, pt: (layer, pt[b, g * pages + t], 0, 0))

    grid_spec = pltpu.PrefetchScalarGridSpec(
        num_scalar_prefetch=1,
        grid=(n_seq, n_pages // pages),
        in_specs=[pl.BlockSpec((rows, cdim), lambda b, g, pt: (rb + b, 0)),
                  pl.BlockSpec((rows, rdim), lambda b, g, pt: (rb + b, 0)),
                  pl.BlockSpec((n_new, cdim), lambda b, g, pt: (b, 0)),
                  pl.BlockSpec((n_new, rdim), lambda b, g, pt: (b, 0))]
        + [c_spec(t) for t in range(pages)] + [kr_spec(t) for t in range(pages)],
        out_specs=pl.BlockSpec((rows, cdim), lambda b, g, pt: (b, 0)),
        scratch_shapes=[pltpu.VMEM((rows, 1), F32), pltpu.VMEM((rows, 1), F32), pltpu.VMEM((rows, cdim), F32)],
    )
    return pl.pallas_call(
        functools.partial(_mla_sample_kernel, scale=scale, pages=pages, heads=heads),
        grid_spec=grid_spec,
        out_shape=jax.ShapeDtypeStruct((n_seq * rows, cdim), BF16),
        compiler_params=_cparams("parallel", "arbitrary"),
        name="mla_sample",
    )(page_table, ql, qr, c_new, kr_new, *([cache_c] * pages), *([cache_kr] * pages))


TM = 1024


def _hgrn_layer(x, norm_g, w_in, lb, gnorm, w_out, s0, *, n_prompt, prompt_len, n_sample, sample_len):
    d = x.shape[1]
    dk = d // HG_HEADS
    p = norm_matmul(x, norm_g, w_in.astype(BF16), tm=TM, tn=512)
    kw = dict(mode="hgrn", heads=HG_HEADS, dk=dk, dv=dk, scale=dk ** -0.5)
    op, sp = lin_attn(p, (lb,), gnorm, None, n_seq=n_prompt, seq_len=prompt_len, row0=0, **kw)
    os_, ss = lin_attn(p, (lb,), gnorm, s0, n_seq=n_sample, seq_len=sample_len, row0=n_prompt * prompt_len, **kw)
    o = jnp.concatenate([op, os_], axis=0)
    return matmul_res(o, w_out.astype(BF16), x, tm=TM, tn=1024, tk=512), sp, ss


def _gla_layer(x, norm_g, w_in, w_gate_up, b_gate, gnorm, w_out, s0, *, n_prompt, prompt_len, n_sample, sample_len):
    d = x.shape[1]
    dk, dv = d // 2 // GLA_HEADS, d // GLA_HEADS
    w_pad = jnp.pad(w_in, ((0, 0), (0, LANES - GLA_GATE_RANK))).astype(BF16)
    p = norm_matmul(x, norm_g, w_pad, tm=TM, tn=w_pad.shape[1] // 7)
    wg = jnp.pad(w_gate_up, ((0, LANES - GLA_GATE_RANK), (0, 0))).astype(BF16)
    kw = dict(mode="gla", heads=GLA_HEADS, dk=dk, dv=dv, scale=dk ** -0.5)
    op, sp = lin_attn(p, (wg, b_gate), gnorm, None, n_seq=n_prompt, seq_len=prompt_len, row0=0, **kw)
    os_, ss = lin_attn(p, (wg, b_gate), gnorm, s0, n_seq=n_sample, seq_len=sample_len,
                       row0=n_prompt * prompt_len, **kw)
    o = jnp.concatenate([op, os_], axis=0)
    return matmul_res(o, w_out.astype(BF16), x, tm=TM, tn=1024, tk=512), sp, ss


def _moba_layer(x, norm_g, w_in, w_out, cache_k, cache_v, page_table, layer, *, n_prompt, prompt_len, n_sample,
                sample_len):
    dh, h, hkv = MB_HEAD_DIM, MB_HEADS, MB_KV_HEADS
    tp = n_prompt * prompt_len
    p = norm_matmul(x, norm_g, w_in.astype(BF16), tm=TM, tn=512)
    k = p[:, h * dh:(h + hkv) * dh]
    v = p[:, (h + hkv) * dh:]
    ap = moba_prompt(p, n_seq=n_prompt, seq_len=prompt_len, heads=h, kv_heads=hkv, dh=dh)
    qs = p[tp:, :h * dh].reshape(n_sample, sample_len, h, dh).transpose(0, 2, 1, 3).reshape(n_sample, h * sample_len, dh)
    ks = k[tp:].reshape(n_sample, sample_len * hkv, dh)
    vs = v[tp:].reshape(n_sample, sample_len * hkv, dh)
    n_layers, n_pool = cache_k.shape[:2]
    ck = cache_k.reshape(n_layers, n_pool, PAGE_SIZE * hkv, dh)
    cv = cache_v.reshape(n_layers, n_pool, PAGE_SIZE * hkv, dh)
    as_ = moba_sample(qs, ks, vs, ck, cv, page_table, layer, kv_heads=hkv, dh=dh)
    as_ = as_.reshape(n_sample, h, sample_len, dh).transpose(0, 2, 1, 3).reshape(n_sample * sample_len, h * dh)
    a = jnp.concatenate([ap, as_.astype(BF16)], axis=0)
    x = matmul_res(a, w_out.astype(BF16), x, tm=TM, tn=1024, tk=512)
    shape_p = (n_prompt, prompt_len, hkv, dh)
    shape_s = (n_sample, sample_len, hkv, dh)
    return x, k[:tp].reshape(shape_p), v[:tp].reshape(shape_p), k[tp:].reshape(shape_s), v[tp:].reshape(shape_s)


def _rope_tables(pos):
    half = MLA_ROPE // 2
    inv = ROPE_THETA ** (-jnp.arange(half, dtype=F32) / half)
    ang = pos.astype(F32)[:, None] * inv[None, :]
    reps = LANES // half
    return jnp.tile(jnp.cos(ang), (1, reps)), jnp.tile(jnp.sin(ang), (1, reps))


def _mla_layer(x, norm_g, w_in, q_norm, w_q_up, kv_norm, w_kv_up, w_out, cache_c, cache_kr, page_table, layer, pos, *,
               n_prompt, prompt_len, n_sample, sample_len):
    t = x.shape[0]
    h, half = MLA_HEADS, MLA_ROPE // 2
    tp = n_prompt * prompt_len
    d = w_in.shape[0]
    qa_w, ckv_w, kr_w = w_in[:, :MLA_Q_RANK], w_in[:, MLA_Q_RANK:MLA_Q_RANK + MLA_KV_RANK], w_in[:, MLA_Q_RANK + MLA_KV_RANK:]
    qpad = 1024 - MLA_Q_RANK
    zeros = lambda n: jnp.zeros((d, n), w_in.dtype)
    w_p = jnp.concatenate([qa_w, zeros(qpad), ckv_w, kr_w[:, :half], zeros(LANES - half), kr_w[:, half:],
                           zeros(LANES - half)], axis=1).astype(BF16)
    p = norm_matmul(x, norm_g, w_p, tm=TM, tn=256)
    w4 = w_q_up.reshape(MLA_Q_RANK, h, MLA_NOPE + MLA_ROPE)
    w_q = jnp.concatenate([w4[:, :, :MLA_NOPE].reshape(MLA_Q_RANK, h * MLA_NOPE),
                           w4[:, :, MLA_NOPE:MLA_NOPE + half].reshape(MLA_Q_RANK, h * half),
                           w4[:, :, MLA_NOPE + half:].reshape(MLA_Q_RANK, h * half)], axis=1).astype(BF16)
    q3 = norm_matmul(p, q_norm, w_q, tm=TM, tn=512)
    c32 = rmsnorm_rows(p, kv_norm, col=1024 // MLA_KV_RANK, tm=TM)
    cos, sin = _rope_tables(pos)
    qr1, qr2 = rope_pairs(q3, cos, sin, col1=h * MLA_NOPE // (h * half), col2=h * MLA_NOPE // (h * half) + 1,
                          width=h * half, tm=TM, out_dtype=BF16)
    kc = (1024 + MLA_KV_RANK) // LANES
    kr1, kr2 = rope_pairs(p, cos, sin, col1=kc, col2=kc + 1, width=LANES, tm=TM, out_dtype=F32)
    kr = jnp.concatenate([kr1[:, :half], kr2[:, :half]], axis=1)
    q_rope = jnp.concatenate([qr1.reshape(t, h, half), qr2.reshape(t, h, half)], axis=-1).reshape(t * h, MLA_ROPE)
    wkv = w_kv_up.reshape(MLA_KV_RANK, h, MLA_NOPE + MLA_V)
    w_uk = wkv[:, :, :MLA_NOPE].transpose(1, 2, 0).astype(BF16)
    w_uv = wkv[:, :, MLA_NOPE:].transpose(1, 0, 2).astype(BF16)
    q_lat = head_matmul(q3, w_uk, tm=TM).reshape(t * h, MLA_KV_RANK)
    scale = (MLA_NOPE + MLA_ROPE) ** -0.5
    c16, kr16 = c32.astype(BF16), kr.astype(BF16)
    op = mla_prompt(q_lat, q_rope, c16, kr16, n_seq=n_prompt, seq_len=prompt_len, heads=h, scale=scale)
    os_ = mla_sample(q_lat, q_rope, c32[tp:], kr[tp:], cache_c, cache_kr, page_table, layer, row0=tp, heads=h,
                     scale=scale)
    o_lat = jnp.concatenate([op, os_], axis=0).reshape(t, h * MLA_KV_RANK)
    o = head_matmul(o_lat, w_uv, tm=TM)
    x = matmul_res(o, w_out.astype(BF16), x, tm=TM, tn=1024, tk=512)
    return (x, c32[:tp].reshape(n_prompt, prompt_len, -1), kr[:tp].reshape(n_prompt, prompt_len, -1),
            c32[tp:].reshape(n_sample, sample_len, -1), kr[tp:].reshape(n_sample, sample_len, -1))


def _dense_ffn(x, norm_g, w_in, w_out):
    act = swiglu_proj(x, norm_g, w_in.astype(BF16)[None], None, tm=TM, tn=512)
    return matmul_res(act, w_out.astype(BF16), x, tm=TM, tn=1024, tk=512)


def _moe_ffn(x, norm_g, wr, w_in, w_out):
    rw, h = router(x, norm_g, wr, tm=512)
    act = swiglu_proj(h, None, w_in.astype(BF16), rw, tm=TM, tn=256)
    e, f, d = w_out.shape
    return matmul_res(act, w_out.astype(BF16).reshape(e * f, d), x, tm=TM, tn=1024, tk=512)


def kernel(x_prompt, x_sample, state_hgrn, state_gla, cache_moba_k, cache_moba_v, cache_mla_latent, cache_mla_krope,
           page_table, norm_mixer, norm_ffn, norm_final, hgrn_w_in, hgrn_lb_logits, hgrn_gnorm, hgrn_w_out, gla_w_in,
           gla_w_gate_up, gla_b_gate, gla_gnorm, gla_w_out, moba_w_in, moba_w_out, mla_w_in, mla_q_norm, mla_w_q_up,
           mla_kv_norm, mla_w_kv_up, mla_w_out, ffn_w_in, ffn_w_out, moe_router, moe_w_in, moe_w_out):
    n_prompt, prompt_len, d = x_prompt.shape
    n_sample, sample_len, _ = x_sample.shape
    past_len = page_table.shape[1] * PAGE_SIZE
    depth = norm_mixer.shape[0]
    dims = dict(n_prompt=n_prompt, prompt_len=prompt_len, n_sample=n_sample, sample_len=sample_len)
    tp = n_prompt * prompt_len
    pos = jnp.concatenate([jnp.tile(jnp.arange(prompt_len, dtype=jnp.int32), n_prompt),
                           jnp.tile(past_len + jnp.arange(sample_len, dtype=jnp.int32), n_sample)])
    lb_all = jnp.cumsum(jax.nn.softmax(hgrn_lb_logits.astype(F32), axis=0), axis=0)
    x = jnp.concatenate([x_prompt.reshape(tp, d), x_sample.reshape(-1, d)], axis=0)
    outs = {k: [] for k in ("hg_p", "hg_s", "gla_p", "gla_s", "mbk_p", "mbv_p", "mbk_s", "mbv_s",
                            "mlc_p", "mlr_p", "mlc_s", "mlr_s")}
    for i in range(depth):
        m, j = i % 4, i // 4
        if m == 0:
            x, sp, ss = _hgrn_layer(x, norm_mixer[i], hgrn_w_in[j], lb_all[i], hgrn_gnorm[j], hgrn_w_out[j],
                                    state_hgrn[j], **dims)
            outs["hg_p"].append(sp)
            outs["hg_s"].append(ss)
        elif m == 1:
            x, sp, ss = _gla_layer(x, norm_mixer[i], gla_w_in[j], gla_w_gate_up[j], gla_b_gate[j], gla_gnorm[j],
                                   gla_w_out[j], state_gla[j], **dims)
            outs["gla_p"].append(sp)
            outs["gla_s"].append(ss)
        elif m == 2:
            x, kp, vp, ks, vs = _moba_layer(x, norm_mixer[i], moba_w_in[j], moba_w_out[j], cache_moba_k, cache_moba_v,
                                            page_table, j, **dims)
            for key, val in zip(("mbk_p", "mbv_p", "mbk_s", "mbv_s"), (kp, vp, ks, vs)):
                outs[key].append(val)
        else:
            x, cp, rp, cs, rs = _mla_layer(x, norm_mixer[i], mla_w_in[j], mla_q_norm[j], mla_w_q_up[j], mla_kv_norm[j],
                                           mla_w_kv_up[j], mla_w_out[j], cache_mla_latent, cache_mla_krope, page_table,
                                           j, pos, **dims)
            for key, val in zip(("mlc_p", "mlr_p", "mlc_s", "mlr_s"), (cp, rp, cs, rs)):
                outs[key].append(val)
        if i % 2 == 0:
            x = _dense_ffn(x, norm_ffn[i], ffn_w_in[i // 2], ffn_w_out[i // 2])
        else:
            x = _moe_ffn(x, norm_ffn[i], moe_router[i // 2], moe_w_in[i // 2], moe_w_out[i // 2])
    y = rmsnorm_rows(x, norm_final, tm=TM)
    return (y[:tp].reshape(x_prompt.shape), y[tp:].reshape(x_sample.shape),
            *(jnp.stack(outs[k]) for k in ("hg_p", "hg_s", "gla_p", "gla_s", "mbk_p", "mbv_p", "mbk_s", "mbv_s",
                                           "mlc_p", "mlr_p", "mlc_s", "mlr_s")))
```

```python
import functools

import jax
import jax.numpy as jnp
from jax import lax
from jax.experimental import pallas as pl
from jax.experimental.pallas import tpu as pltpu

F32 = jnp.float32
BF16 = jnp.bfloat16
NEG_INF = float("-inf")

EPS = 1e-6
PAGE_SIZE = 128
HG_HEADS = 16
GLA_HEADS = 4
GLA_GATE_RANK = 16
GLA_GATE_NORM = 16.0
MB_HEADS = 16
MB_KV_HEADS = 4
MB_HEAD_DIM = 128
MB_BLOCK = 256
MB_TOPK = 3
MLA_HEADS = 16
MLA_Q_RANK = 768
MLA_KV_RANK = 512
MLA_NOPE = 128
MLA_ROPE = 64
MLA_V = 128
ROPE_THETA = 10000.0

LANES = 128
SUBLANES = 8
VMEM_LIMIT = 48 * 2 ** 20
LA_CHUNK = 64
PAGES_PER_STEP = 8
STATE_BLOCK_BYTES = 4 * 2 ** 20
MLA_TQ = 128
MLA_TK = 512
MLA_GROUP_LANES = 256


def _cparams(*sem):
    return pltpu.CompilerParams(dimension_semantics=sem, vmem_limit_bytes=VMEM_LIMIT)


def _dot(a, b):
    return jnp.dot(a, b, preferred_element_type=F32)


def _dot_nt(a, b):
    return lax.dot_general(a, b, (((1,), (1,)), ((), ())), preferred_element_type=F32)


def _dot_tn(a, b):
    return lax.dot_general(a, b, (((0,), (0,)), ((), ())), preferred_element_type=F32)


def _norm_rows(x, g):
    return x * lax.rsqrt(jnp.mean(x * x, axis=-1, keepdims=True) + EPS) * g


def _lane_col(x, j):
    lane = lax.broadcasted_iota(jnp.int32, x.shape, 1)
    return jnp.sum(jnp.where(lane == j, x, 0.0), axis=-1, keepdims=True)


def _round_up(n, m):
    return -(-n // m) * m


def _norm_matmul_kernel(x_ref, g_ref, w_ref, o_ref, h_ref):
    @pl.when(pl.program_id(1) == 0)
    def _():
        h_ref[...] = _norm_rows(x_ref[...], g_ref[...]).astype(BF16)

    o_ref[...] = _dot(h_ref[...], w_ref[...]).astype(o_ref.dtype)


def norm_matmul(x, g, w, layer, *, tm, tn, out_dtype=F32):
    m = x.shape[0]
    _, k, n = w.shape
    return pl.pallas_call(
        _norm_matmul_kernel,
        grid=(m // tm, n // tn),
        in_specs=[pl.BlockSpec((tm, k), lambda i, j: (i, 0)),
                  pl.BlockSpec((1, k), lambda i, j: (0, 0)),
                  pl.BlockSpec((None, k, tn), lambda i, j: (layer, 0, j))],
        out_specs=pl.BlockSpec((tm, tn), lambda i, j: (i, j)),
        out_shape=jax.ShapeDtypeStruct((m, n), out_dtype),
        scratch_shapes=[pltpu.VMEM((tm, k), BF16)],
        compiler_params=_cparams("parallel", "arbitrary"),
        name="norm_matmul",
    )(x, g.reshape(1, k).astype(F32), w)


def _swiglu_kernel(*refs, has_norm, has_scale, nf):
    refs = list(refs)
    x_ref = refs.pop(0)
    g_ref = refs.pop(0) if has_norm else None
    wg_ref, wu_ref = refs.pop(0), refs.pop(0)
    rw_ref = refs.pop(0) if has_scale else None
    o_ref = refs.pop(0)
    if has_norm:
        h_ref = refs.pop(0)

        @pl.when(pl.program_id(1) == 0)
        def _():
            h_ref[...] = _norm_rows(x_ref[...], g_ref[...]).astype(BF16)

        h = h_ref[...]
    else:
        h = x_ref[...]
    a = _dot(h, wg_ref[...])
    u = _dot(h, wu_ref[...])
    act = a * jax.nn.sigmoid(a) * u
    if has_scale:
        act = act * _lane_col(rw_ref[...], pl.program_id(1) // nf)
    o_ref[...] = act.astype(o_ref.dtype)


def swiglu_proj(x, g, w, layer, rw, *, tm, tn):
    m, k = x.shape
    _, e, _, f2 = w.shape
    f = f2 // 2
    nf = f // tn
    has_norm, has_scale = g is not None, rw is not None
    in_specs = [pl.BlockSpec((tm, k), lambda i, j: (i, 0))]
    args = [x]
    if has_norm:
        in_specs.append(pl.BlockSpec((1, k), lambda i, j: (0, 0)))
        args.append(g.reshape(1, k).astype(F32))
    in_specs += [pl.BlockSpec((None, None, k, tn), lambda i, j: (layer, j // nf, 0, j % nf)),
                 pl.BlockSpec((None, None, k, tn), lambda i, j: (layer, j // nf, 0, nf + j % nf))]
    args += [w, w]
    if has_scale:
        in_specs.append(pl.BlockSpec((tm, LANES), lambda i, j: (i, 0)))
        args.append(rw)
    return pl.pallas_call(
        functools.partial(_swiglu_kernel, has_norm=has_norm, has_scale=has_scale, nf=nf),
        grid=(m // tm, e * nf),
        in_specs=in_specs,
        out_specs=pl.BlockSpec((tm, tn), lambda i, j: (i, j)),
        out_shape=jax.ShapeDtypeStruct((m, e * f), BF16),
        scratch_shapes=[pltpu.VMEM((tm, k), BF16)] if has_norm else [],
        compiler_params=_cparams("parallel", "arbitrary"),
        name="swiglu_proj",
    )(*args)


def _matmul_res_kernel(a_ref, w_ref, r_ref, o_ref):
    @pl.when(pl.program_id(2) == 0)
    def _():
        o_ref[...] = r_ref[...]

    o_ref[...] += _dot(a_ref[...], w_ref[...])


def matmul_res(a, w, layer, res, *, tm, tn, tk):
    m, k = a.shape
    n = w.shape[2]
    return pl.pallas_call(
        _matmul_res_kernel,
        grid=(m // tm, n // tn, k // tk),
        in_specs=[pl.BlockSpec((tm, tk), lambda i, j, kk: (i, kk)),
                  pl.BlockSpec((None, tk, tn), lambda i, j, kk: (layer, kk, j)),
                  pl.BlockSpec((tm, tn), lambda i, j, kk: (i, j))],
        out_specs=pl.BlockSpec((tm, tn), lambda i, j, kk: (i, j)),
        out_shape=jax.ShapeDtypeStruct((m, n), F32),
        compiler_params=_cparams("parallel", "parallel", "arbitrary"),
        name="matmul_res",
    )(a, w, res)


def _rmsnorm_kernel(x_ref, g_ref, o_ref):
    o_ref[...] = _norm_rows(x_ref[...], g_ref[...]).astype(o_ref.dtype)


def rmsnorm_rows(x, g, *, col=0, tm, out_dtype=F32):
    m = x.shape[0]
    k = g.shape[-1]
    return pl.pallas_call(
        _rmsnorm_kernel,
        grid=(m // tm,),
        in_specs=[pl.BlockSpec((tm, k), lambda i: (i, col)),
                  pl.BlockSpec((1, k), lambda i: (0, 0))],
        out_specs=pl.BlockSpec((tm, k), lambda i: (i, 0)),
        out_shape=jax.ShapeDtypeStruct((m, k), out_dtype),
        compiler_params=_cparams("parallel"),
        name="rmsnorm_rows",
    )(x, g.reshape(1, k).astype(F32))


def _router_kernel(x_ref, g_ref, wr_ref, rw_ref, h_ref, *, n_experts):
    h = _norm_rows(x_ref[...], g_ref[...])
    h_ref[...] = h.astype(BF16)
    logits = jnp.dot(h, wr_ref[...], preferred_element_type=F32, precision=lax.Precision.HIGHEST)
    lane = lax.broadcasted_iota(jnp.int32, logits.shape, 1)
    l1 = jnp.where(lane < n_experts, logits, NEG_INF)
    m1 = jnp.max(l1, axis=-1, keepdims=True)
    i1 = jnp.min(jnp.where(l1 == m1, lane, LANES), axis=-1, keepdims=True)
    l2 = jnp.where(lane == i1, NEG_INF, l1)
    m2 = jnp.max(l2, axis=-1, keepdims=True)
    i2 = jnp.min(jnp.where(l2 == m2, lane, LANES), axis=-1, keepdims=True)
    e2 = jnp.exp(m2 - m1)
    den = 1.0 + e2
    rw_ref[...] = jnp.where(lane == i1, 1.0 / den, 0.0) + jnp.where(lane == i2, e2 / den, 0.0)


def router(x, g, wr, *, tm):
    m, k = x.shape
    n_experts = wr.shape[1]
    wr_pad = jnp.pad(wr.astype(F32), ((0, 0), (0, LANES - n_experts)))
    return pl.pallas_call(
        functools.partial(_router_kernel, n_experts=n_experts),
        grid=(m // tm,),
        in_specs=[pl.BlockSpec((tm, k), lambda i: (i, 0)),
                  pl.BlockSpec((1, k), lambda i: (0, 0)),
                  pl.BlockSpec((k, LANES), lambda i: (0, 0))],
        out_specs=[pl.BlockSpec((tm, LANES), lambda i: (i, 0)),
                   pl.BlockSpec((tm, k), lambda i: (i, 0))],
        out_shape=[jax.ShapeDtypeStruct((m, LANES), F32), jax.ShapeDtypeStruct((m, k), BF16)],
        compiler_params=_cparams("parallel"),
        name="router",
    )(x, g.reshape(1, k).astype(F32), wr_pad)


def _head_matmul_kernel(a_ref, w_ref, o_ref):
    o_ref[...] = _dot(a_ref[...].astype(BF16), w_ref[...]).astype(o_ref.dtype)


def head_matmul(a, w, *, tm, out_dtype=BF16):
    m = a.shape[0]
    h, k, n = w.shape
    return pl.pallas_call(
        _head_matmul_kernel,
        grid=(m // tm, h),
        in_specs=[pl.BlockSpec((tm, k), lambda i, j: (i, j)),
                  pl.BlockSpec((None, k, n), lambda i, j: (j, 0, 0))],
        out_specs=pl.BlockSpec((tm, n), lambda i, j: (i, j)),
        out_shape=jax.ShapeDtypeStruct((m, h * n), out_dtype),
        compiler_params=_cparams("parallel", "arbitrary"),
        name="head_matmul",
    )(a, w)


def _rope_kernel(x1_ref, x2_ref, cos_ref, sin_ref, r1_ref, r2_ref, *, reps):
    cos, sin = cos_ref[...], sin_ref[...]
    if reps > 1:
        cos = jnp.concatenate([cos] * reps, axis=-1)
        sin = jnp.concatenate([sin] * reps, axis=-1)
    x1, x2 = x1_ref[...], x2_ref[...]
    r1_ref[...] = (x1 * cos - x2 * sin).astype(r1_ref.dtype)
    r2_ref[...] = (x1 * sin + x2 * cos).astype(r2_ref.dtype)


def rope_pairs(x, cos, sin, *, col1, col2, width, tm):
    m = x.shape[0]
    return pl.pallas_call(
        functools.partial(_rope_kernel, reps=width // LANES),
        grid=(m // tm,),
        in_specs=[pl.BlockSpec((tm, width), lambda i: (i, col1)),
                  pl.BlockSpec((tm, width), lambda i: (i, col2)),
                  pl.BlockSpec((tm, LANES), lambda i: (i, 0)),
                  pl.BlockSpec((tm, LANES), lambda i: (i, 0))],
        out_specs=[pl.BlockSpec((tm, width), lambda i: (i, 0))] * 2,
        out_shape=[jax.ShapeDtypeStruct((m, width), F32)] * 2,
        compiler_params=_cparams("parallel"),
        name="rope_pairs",
    )(x, x, cos, sin)


def _cumsum_rows(x, c):
    row = lax.broadcasted_iota(jnp.int32, x.shape, 0)
    sh = 1
    while sh < c:
        x = x + jnp.where(row >= sh, pltpu.roll(x, sh, 0), 0.0)
        sh *= 2
    return x


def _gla_chunk(q, k, v, lf, s_prev, c):
    dk = q.shape[1]
    b = _cumsum_rows(lf, c)
    rid = lax.broadcasted_iota(jnp.int32, (c, 1), 0) % SUBLANES
    o = jnp.sum(q * k, axis=-1, keepdims=True) * v
    for d in range(1, SUBLANES):
        kd, bd, vd = pltpu.roll(k, d, 0), pltpu.roll(b, d, 0), pltpu.roll(v, d, 0)
        w = jnp.sum(q * kd * jnp.exp(jnp.minimum(b - bd, 0.0)), axis=-1, keepdims=True)
        o = o + jnp.where(rid >= d, w, 0.0) * vd
    if c > SUBLANES:
        t = lax.broadcasted_iota(jnp.int32, (c, c), 0)
        s = lax.broadcasted_iota(jnp.int32, (c, c), 1)
        a = jnp.zeros((c, c), F32)
        blk = 2 * SUBLANES
        while blk <= c:
            half = blk // 2
            ref = jnp.concatenate(
                [jnp.broadcast_to(b[i * blk + half - 1:i * blk + half, :], (blk, dk)) for i in range(c // blk)], axis=0)
            qt = (q * jnp.exp(jnp.minimum(b - ref, 0.0))).astype(BF16)
            kt = (k * jnp.exp(jnp.minimum(ref - b, 0.0))).astype(BF16)
            pair = (t // blk == s // blk) & (t % blk >= half) & (s % blk < half)
            a = a + jnp.where(pair, _dot_nt(qt, kt), 0.0)
            blk *= 2
        o = o + _dot(a.astype(BF16), v.astype(BF16))
    o = o + _dot((q * jnp.exp(b)).astype(BF16), s_prev.astype(BF16))
    b_end = b[c - 1:c, :]
    kt = (k * jnp.exp(b_end - b)).astype(BF16)
    decay = jnp.exp(jnp.sum(lf.T, axis=-1, keepdims=True))
    s_new = decay * s_prev + _dot_tn(kt, v.astype(BF16))
    return o, s_new


def _lin_attn_kernel(*refs, mode, sample, c, n_inner, scale):
    refs = list(refs)
    if mode == "hgrn":
        q_ref, f_ref, v_ref, g_ref, lb_ref, gn_ref = refs[:6]
        refs = refs[6:]
    else:
        q_ref, k_ref, v_ref, g_ref, gl_ref, wg_ref, bg_ref, gn_ref = refs[:8]
        refs = refs[8:]
    if sample:
        s0_ref, o_ref, so_ref = refs
    else:
        o_ref, so_ref, s_scr = refs

    def prep(rows):
        if mode == "hgrn":
            qr, fr, lb = q_ref[rows, :], f_ref[rows, :], lb_ref[...]
            q = qr * jax.nn.sigmoid(qr) * scale
            lf = jnp.log(lb + (1.0 - lb) * jax.nn.sigmoid(fr))
            k = (1.0 - lb) * jax.nn.sigmoid(-fr)
        else:
            q = q_ref[rows, :] * scale
            k = k_ref[rows, :]
            z = _dot(gl_ref[rows, :].astype(BF16), wg_ref[...]) + bg_ref[...]
            lf = -(jnp.maximum(-z, 0.0) + jnp.log1p(jnp.exp(-jnp.abs(z)))) * (1.0 / GLA_GATE_NORM)
        return q, k, v_ref[rows, :], lf

    def finish(rows, o):
        g = g_ref[rows, :]
        o_ref[rows, :] = (_norm_rows(o, gn_ref[...]) * (g * jax.nn.sigmoid(g))).astype(o_ref.dtype)

    if sample:
        def body(si, carry):
            rows = pl.ds(pl.multiple_of(si * c, c), c)
            q, k, v, lf = prep(rows)
            o, s_new = _gla_chunk(q, k, v, lf, s0_ref[si], c)
            so_ref[si] = s_new
            finish(rows, o)
            return carry
    else:
        @pl.when(pl.program_id(2) == 0)
        def _():
            s_scr[...] = jnp.zeros_like(s_scr)

        def body(ci, carry):
            rows = pl.ds(pl.multiple_of(ci * c, c), c)
            q, k, v, lf = prep(rows)
            o, s_new = _gla_chunk(q, k, v, lf, s_scr[...], c)
            s_scr[...] = s_new
            finish(rows, o)
            return carry

    lax.fori_loop(0, n_inner, body, 0, unroll=2 if sample else 1)

    if not sample:
        @pl.when(pl.program_id(2) == pl.num_programs(2) - 1)
        def _():
            so_ref[...] = s_scr[...]


def lin_attn(p, extra, gnorm, s0, *, mode, heads, dk, dv, n_seq, seq_len, row0, scale):
    sample = s0 is not None
    if sample:
        c = seq_len
        nb = max(1, min(n_seq, STATE_BLOCK_BYTES // (dk * dv * 4)))
        tb, n_inner = nb * c, nb
        grid = (n_seq // nb, heads, 1)
        rb = row0 // tb
        row = lambda b, h, i: rb + b
    else:
        c, tb = LA_CHUNK, 4 * LA_CHUNK
        n_inner = tb // c
        nt = seq_len // tb
        grid = (n_seq, heads, nt)
        rb = row0 // tb
        row = lambda b, h, i: rb + b * nt + i
    if mode == "hgrn":
        lb, = extra
        nh = heads
        in_specs = [pl.BlockSpec((tb, dk), lambda b, h, i: (row(b, h, i), h)),
                    pl.BlockSpec((tb, dk), lambda b, h, i: (row(b, h, i), nh + h)),
                    pl.BlockSpec((tb, dv), lambda b, h, i: (row(b, h, i), 2 * nh + h)),
                    pl.BlockSpec((tb, dv), lambda b, h, i: (row(b, h, i), 3 * nh + h)),
                    pl.BlockSpec((1, dk), lambda b, h, i: (0, h)),
                    pl.BlockSpec((1, dv), lambda b, h, i: (0, 0))]
        args = [p, p, p, p, lb.reshape(1, -1).astype(F32), gnorm.reshape(1, dv).astype(F32)]
    else:
        wg, bg = extra
        nh = heads
        voff = 2 * heads * dk // dv
        gloff = (2 * heads * dk + 2 * heads * dv) // LANES
        in_specs = [pl.BlockSpec((tb, dk), lambda b, h, i: (row(b, h, i), h)),
                    pl.BlockSpec((tb, dk), lambda b, h, i: (row(b, h, i), nh + h)),
                    pl.BlockSpec((tb, dv), lambda b, h, i: (row(b, h, i), voff + h)),
                    pl.BlockSpec((tb, dv), lambda b, h, i: (row(b, h, i), voff + nh + h)),
                    pl.BlockSpec((tb, LANES), lambda b, h, i: (row(b, h, i), gloff)),
                    pl.BlockSpec((LANES, dk), lambda b, h, i: (0, h)),
                    pl.BlockSpec((1, dk), lambda b, h, i: (0, h)),
                    pl.BlockSpec((1, dv), lambda b, h, i: (0, 0))]
        args = [p, p, p, p, p, wg, bg.reshape(1, -1).astype(F32), gnorm.reshape(1, dv).astype(F32)]
    out_specs = [pl.BlockSpec((tb, dv), lambda b, h, i: (row(b, h, i) - rb, h))]
    out_shape = [jax.ShapeDtypeStruct((n_seq * seq_len, heads * dv), BF16)]
    scratch = []
    if sample:
        in_specs.append(pl.BlockSpec((nb, None, dk, dv), lambda b, h, i: (b, h, 0, 0)))
        args.append(s0)
        out_specs.append(pl.BlockSpec((nb, None, dk, dv), lambda b, h, i: (b, h, 0, 0)))
    else:
        out_specs.append(pl.BlockSpec((None, None, dk, dv), lambda b, h, i: (b, h, 0, 0)))
        scratch.append(pltpu.VMEM((dk, dv), F32))
    out_shape.append(jax.ShapeDtypeStruct((n_seq, heads, dk, dv), F32))
    return pl.pallas_call(
        functools.partial(_lin_attn_kernel, mode=mode, sample=sample, c=c, n_inner=n_inner, scale=scale),
        grid=grid,
        in_specs=in_specs,
        out_specs=out_specs,
        out_shape=out_shape,
        scratch_shapes=scratch,
        compiler_params=_cparams("parallel", "parallel", "arbitrary"),
        name=f"lin_attn_{mode}_{'sample' if sample else 'prompt'}",
    )(*args)


def _moba_weights(gate, m_all, n_prev, own, n_static):
    ridx = lax.broadcasted_iota(jnp.int32, gate.shape, 0)
    cnt = jnp.zeros(gate.shape, F32)
    for jp in range(n_static):
        row = gate[jp:jp + 1, :]
        beats = (row > gate) | ((row == gate) & (jp < ridx))
        cnt = cnt + jnp.where(beats & (jp < n_prev), 1.0, 0.0)
    sel = ((cnt < MB_TOPK) & (ridx < n_prev)) | (ridx == own)
    mv = jnp.where(sel, m_all, NEG_INF)
    return jnp.where(sel, jnp.exp(mv - jnp.max(mv, axis=0, keepdims=True)), 0.0)


def _init_partials(m_scr, l_scr, g_scr):
    m_scr[...] = jnp.full(m_scr.shape, NEG_INF, F32)
    g_scr[...] = jnp.full(g_scr.shape, NEG_INF, F32)
    l_scr[...] = jnp.zeros(l_scr.shape, F32)


def _moba_prompt_kernel(q_ref, k_ref, v_ref, o_ref, qt_scr, m_scr, l_scr, g_scr, w_scr, o_scr, *, scale, group, n_blk):
    i = pl.program_id(2)
    dh = k_ref.shape[1]
    for hh in range(group):
        qt_scr[hh] = (q_ref[:, hh * dh:(hh + 1) * dh] * scale).T.astype(BF16)
    _init_partials(m_scr, l_scr, g_scr)
    kidx = lax.broadcasted_iota(jnp.int32, (MB_BLOCK, MB_BLOCK), 0)
    qidx = lax.broadcasted_iota(jnp.int32, (MB_BLOCK, MB_BLOCK), 1)
    causal = kidx <= qidx

    def blk(j, carry):
        rows = pl.ds(pl.multiple_of(j * MB_BLOCK, MB_BLOCK), MB_BLOCK)
        kj, vj = k_ref[rows, :].astype(BF16), v_ref[rows, :].astype(BF16)
        keep = causal | (j < i)
        for hh in range(group):
            s = _dot(kj, qt_scr[hh])
            g_scr[hh, pl.ds(j, 1), :] = jnp.sum(s, axis=0, keepdims=True)
            s = jnp.where(keep, s, NEG_INF)
            m = jnp.max(s, axis=0, keepdims=True)
            p = jnp.exp(s - m)
            m_scr[hh, pl.ds(j, 1), :] = m
            l_scr[hh, pl.ds(j, 1), :] = jnp.sum(p, axis=0, keepdims=True)
            o_scr[hh, j] = _dot_tn(vj, p.astype(BF16))
        return carry

    lax.fori_loop(0, i + 1, blk, 0)
    for hh in range(group):
        w = _moba_weights(g_scr[hh], m_scr[hh], i, i, n_blk)
        den = jnp.sum(w * l_scr[hh], axis=0, keepdims=True)
        w_scr[...] = w

        def comb(j, acc):
            return acc + w_scr[pl.ds(j, 1), :] * o_scr[hh, j]

        acc = lax.fori_loop(0, i + 1, comb, jnp.zeros((dh, MB_BLOCK), F32))
        o_ref[:, hh * dh:(hh + 1) * dh] = (acc / den).T.astype(o_ref.dtype)


def moba_prompt(p, *, n_seq, seq_len, heads, kv_heads, dh):
    group = heads // kv_heads
    nq = seq_len // MB_BLOCK
    nbp = _round_up(nq, SUBLANES)
    koff = heads
    voff = heads + kv_heads
    return pl.pallas_call(
        functools.partial(_moba_prompt_kernel, scale=dh ** -0.5, group=group, n_blk=nq),
        grid=(n_seq, kv_heads, nq),
        in_specs=[pl.BlockSpec((MB_BLOCK, group * dh), lambda b, g, i: (b * nq + i, g)),
                  pl.BlockSpec((seq_len, dh), lambda b, g, i: (b, koff + g)),
                  pl.BlockSpec((seq_len, dh), lambda b, g, i: (b, voff + g))],
        out_specs=pl.BlockSpec((MB_BLOCK, group * dh), lambda b, g, i: (b * nq + i, g)),
        out_shape=jax.ShapeDtypeStruct((n_seq * seq_len, heads * dh), BF16),
        scratch_shapes=[pltpu.VMEM((group, dh, MB_BLOCK), BF16)]
        + [pltpu.VMEM((group, nbp, MB_BLOCK), F32)] * 3
        + [pltpu.VMEM((nbp, MB_BLOCK), F32), pltpu.VMEM((group, nq, dh, MB_BLOCK), F32)],
        compiler_params=_cparams("parallel", "parallel", "arbitrary"),
        name="moba_prompt",
    )(p, p, p)


def _moba_sample_kernel(pt_ref, qbd_ref, kn_ref, vn_ref, *refs, pages, kv_heads, n_new):
    del pt_ref
    k_refs, v_refs = refs[:pages], refs[pages:2 * pages]
    o_ref, m_scr, l_scr, g_scr, o_scr = refs[2 * pages:]
    g = pl.program_id(1)
    rows, dh = o_ref.shape
    rpk = rows // kv_heads
    ppb = MB_BLOCK // PAGE_SIZE
    bps = pages // ppb

    def scores(k_ref, n):
        parts = [_dot(k_ref[pl.ds(gk, n, stride=kv_heads), :].astype(BF16), qbd_ref[gk * dh:(gk + 1) * dh, :])
                 for gk in range(kv_heads)]
        return functools.reduce(jnp.add, parts)

    def values(v_ref, p, n):
        lane_g = lax.broadcasted_iota(jnp.int32, p.shape, 1) // rpk
        parts = [_dot_tn(v_ref[pl.ds(gk, n, stride=kv_heads), :].astype(BF16),
                         jnp.where(lane_g == gk, p, 0.0).astype(BF16)) for gk in range(kv_heads)]
        return functools.reduce(jnp.add, parts)

    @pl.when(g == 0)
    def _():
        _init_partials(m_scr, l_scr, g_scr)

    for bb in range(bps):
        j = g * bps + bb
        ss = [scores(k_refs[bb * ppb + t], PAGE_SIZE) for t in range(ppb)]
        g_scr[pl.ds(j, 1), :] = functools.reduce(jnp.add, [jnp.sum(s, axis=0, keepdims=True) for s in ss])
        m = functools.reduce(jnp.maximum, [jnp.max(s, axis=0, keepdims=True) for s in ss])
        ps = [jnp.exp(s - m) for s in ss]
        m_scr[pl.ds(j, 1), :] = m
        l_scr[pl.ds(j, 1), :] = functools.reduce(jnp.add, [jnp.sum(p, axis=0, keepdims=True) for p in ps])
        o_scr[j] = functools.reduce(jnp.add, [values(v_refs[bb * ppb + t], ps[t], PAGE_SIZE) for t in range(ppb)])

    @pl.when(g == pl.num_programs(1) - 1)
    def _():
        n_prev = o_scr.shape[0] - 1
        s = scores(kn_ref, n_new)
        tok = lax.broadcasted_iota(jnp.int32, s.shape, 0)
        qi = lax.broadcasted_iota(jnp.int32, s.shape, 1) % n_new
        s = jnp.where(tok <= qi, s, NEG_INF)
        m = jnp.max(s, axis=0, keepdims=True)
        p = jnp.exp(s - m)
        m_scr[n_prev:n_prev + 1, :] = m
        l_scr[n_prev:n_prev + 1, :] = jnp.sum(p, axis=0, keepdims=True)
        o_scr[n_prev] = values(vn_ref, p, n_new)
        w = _moba_weights(g_scr[...], m_scr[...], n_prev, n_prev, n_prev)
        den = jnp.sum(w * l_scr[...], axis=0, keepdims=True)
        acc = jnp.zeros((dh, rows), F32)
        for jj in range(n_prev + 1):
            acc = acc + w[jj:jj + 1, :] * o_scr[jj]
        o_ref[...] = (acc / den).T.astype(o_ref.dtype)


def moba_sample(qbd, k_new, v_new, cache_k, cache_v, page_table, layer, *, kv_heads, dh):
    n_seq, _, rows = qbd.shape
    n_new = k_new.shape[1] // kv_heads
    n_pages = page_table.shape[1]
    pages = PAGES_PER_STEP
    n_prev = n_pages * PAGE_SIZE // MB_BLOCK
    nbp = _round_up(n_prev + 1, SUBLANES)
    assert n_pages % pages == 0 and pages % (MB_BLOCK // PAGE_SIZE) == 0
    assert n_new <= MB_BLOCK

    def page_spec(t):
        return pl.BlockSpec((None, None, PAGE_SIZE * kv_heads, dh),
                            lambda b, g, pt: (layer, pt[b, g * pages + t], 0, 0))

    grid_spec = pltpu.PrefetchScalarGridSpec(
        num_scalar_prefetch=1,
        grid=(n_seq, n_pages // pages),
        in_specs=[pl.BlockSpec((None, kv_heads * dh, rows), lambda b, g, pt: (b, 0, 0)),
                  pl.BlockSpec((None, n_new * kv_heads, dh), lambda b, g, pt: (b, 0, 0)),
                  pl.BlockSpec((None, n_new * kv_heads, dh), lambda b, g, pt: (b, 0, 0))]
        + [page_spec(t) for t in range(pages)] * 2,
        out_specs=pl.BlockSpec((None, rows, dh), lambda b, g, pt: (b, 0, 0)),
        scratch_shapes=[pltpu.VMEM((nbp, rows), F32)] * 3 + [pltpu.VMEM((n_prev + 1, dh, rows), F32)],
    )
    return pl.pallas_call(
        functools.partial(_moba_sample_kernel, pages=pages, kv_heads=kv_heads, n_new=n_new),
        grid_spec=grid_spec,
        out_shape=jax.ShapeDtypeStruct((n_seq, rows, dh), F32),
        compiler_params=_cparams("parallel", "arbitrary"),
        name="moba_sample",
    )(page_table, qbd, k_new, v_new, *([cache_k] * pages), *([cache_v] * pages))


def _flash_init(m_scr, l_scr, acc_scr):
    m_scr[...] = jnp.full(m_scr.shape, NEG_INF, F32)
    l_scr[...] = jnp.zeros(l_scr.shape, F32)
    acc_scr[...] = jnp.zeros(acc_scr.shape, F32)


def _flash_update(s_list, pv, m_scr, l_scr, acc_scr):
    m_old = m_scr[...]
    m_new = functools.reduce(jnp.maximum, [jnp.max(s, axis=0, keepdims=True) for s in s_list] + [m_old])
    alpha = jnp.exp(m_old - m_new)
    ps = [jnp.exp(s - m_new) for s in s_list]
    l_scr[...] = alpha * l_scr[...] + functools.reduce(jnp.add, [jnp.sum(p, axis=0, keepdims=True) for p in ps])
    acc_scr[...] = alpha * acc_scr[...] + functools.reduce(
        jnp.add, [pv(idx, p.astype(BF16)) for idx, p in enumerate(ps)])
    m_scr[...] = m_new


def _mla_prompt_kernel(q_ref, qrt_ref, c_ref, ct_ref, kr_ref, wuk_ref, wuv_ref, o_ref, qlt_scr, m_scr, l_scr, acc_scr,
                       *, tq, tk, heads):
    i = pl.program_id(1)
    nope, vdim = wuk_ref.shape[2], wuv_ref.shape[2]
    for h in range(heads):
        qh = q_ref[:, h * nope:(h + 1) * nope].astype(BF16)
        qlt_scr[:, h * tq:(h + 1) * tq] = _dot_nt(wuk_ref[h], qh).astype(BF16)
    gw = m_scr.shape[1]
    hpg = gw // tq
    kidx = lax.broadcasted_iota(jnp.int32, (tk, gw), 0)
    qpos = i * tq + lax.broadcasted_iota(jnp.int32, (tk, gw), 1) % tq
    n_kb = ((i + 1) * tq + tk - 1) // tk
    for cg in range(heads // hpg):
        lanes = slice(cg * gw, (cg + 1) * gw)
        _flash_init(m_scr, l_scr, acc_scr)

        def body(j, carry):
            r0 = pl.multiple_of(j * tk, tk)
            s = _dot(c_ref[pl.ds(r0, tk), :], qlt_scr[:, lanes]) + _dot(kr_ref[pl.ds(r0, tk), :], qrt_ref[:, lanes])
            s = jnp.where(r0 + kidx <= qpos, s, NEG_INF)
            _flash_update([s], lambda idx, p: _dot(ct_ref[j], p), m_scr, l_scr, acc_scr)
            return carry

        lax.fori_loop(0, n_kb, body, 0)
        ot = (acc_scr[...] / l_scr[...]).astype(BF16)
        for hh in range(hpg):
            h = cg * hpg + hh
            o_ref[:, h * vdim:(h + 1) * vdim] = _dot_tn(ot[:, hh * tq:(hh + 1) * tq], wuv_ref[h]).astype(o_ref.dtype)


def mla_prompt(q, qrt, c, ct, kr, wuk, wuv, *, n_seq, seq_len):
    tq, tk = MLA_TQ, MLA_TK
    nq = seq_len // tq
    heads, cdim, nope = wuk.shape
    vdim = wuv.shape[2]
    rdim = kr.shape[1]
    return pl.pallas_call(
        functools.partial(_mla_prompt_kernel, tq=tq, tk=tk, heads=heads),
        grid=(n_seq, nq),
        in_specs=[pl.BlockSpec((tq, heads * nope), lambda b, i: (b * nq + i, 0)),
                  pl.BlockSpec((None, rdim, heads * tq), lambda b, i: (b * nq + i, 0, 0)),
                  pl.BlockSpec((seq_len, cdim), lambda b, i: (b, 0)),
                  pl.BlockSpec((None, seq_len // tk, cdim, tk), lambda b, i: (b, 0, 0, 0)),
                  pl.BlockSpec((seq_len, rdim), lambda b, i: (b, 0)),
                  pl.BlockSpec((heads, cdim, nope), lambda b, i: (0, 0, 0)),
                  pl.BlockSpec((heads, cdim, vdim), lambda b, i: (0, 0, 0))],
        out_specs=pl.BlockSpec((tq, heads * vdim), lambda b, i: (b * nq + i, 0)),
        out_shape=jax.ShapeDtypeStruct((n_seq * seq_len, heads * vdim), BF16),
        scratch_shapes=[pltpu.VMEM((cdim, heads * tq), BF16), pltpu.VMEM((1, MLA_GROUP_LANES), F32),
                        pltpu.VMEM((1, MLA_GROUP_LANES), F32), pltpu.VMEM((cdim, MLA_GROUP_LANES), F32)],
        compiler_params=_cparams("parallel", "arbitrary"),
        name="mla_prompt",
    )(q, qrt, c, ct, kr, wuk, wuv)


def _mla_sample_kernel(pt_ref, qlt_ref, qrt_ref, cn_ref, krn_ref, *refs, pages, heads):
    del pt_ref
    c_refs, krt_refs = refs[:pages], refs[pages:2 * pages]
    o_ref, m_scr, l_scr, acc_scr = refs[2 * pages:]
    g = pl.program_id(1)

    @pl.when(g == 0)
    def _():
        _flash_init(m_scr, l_scr, acc_scr)

    qlt, qrt = qlt_ref[...], qrt_ref[...]
    cs = [c_refs[t][...].astype(BF16) for t in range(pages)]
    ss = [_dot(cs[t], qlt) + _dot_tn(krt_refs[t][...].astype(BF16), qrt) for t in range(pages)]
    _flash_update(ss, lambda idx, p: _dot_tn(cs[idx], p), m_scr, l_scr, acc_scr)

    @pl.when(g == pl.num_programs(1) - 1)
    def _():
        cn = cn_ref[...].astype(BF16)
        s = _dot(cn, qlt) + _dot(krn_ref[...].astype(BF16), qrt)
        tok = lax.broadcasted_iota(jnp.int32, s.shape, 0)
        qi = lax.broadcasted_iota(jnp.int32, s.shape, 1) // heads
        s = jnp.where(tok <= qi, s, NEG_INF)
        _flash_update([s], lambda idx, p: _dot_tn(cn, p), m_scr, l_scr, acc_scr)
        o_ref[...] = (acc_scr[...] / l_scr[...]).T.astype(o_ref.dtype)


def mla_sample(qlt, qrt, c_new, kr_new, cache_c, cache_krt, page_table, layer, *, heads):
    n_seq, n_pages = page_table.shape
    pages = PAGES_PER_STEP
    n_new = c_new.shape[0] // n_seq
    rows = n_new * heads
    cdim, rdim = c_new.shape[1], kr_new.shape[1]
    assert n_pages % pages == 0

    def c_spec(t):
        return pl.BlockSpec((None, None, PAGE_SIZE, cdim), lambda b, g, pt: (layer, pt[b, g * pages + t], 0, 0))

    def kr_spec(t):
        return pl.BlockSpec((None, None, rdim, PAGE_SIZE), lambda b, g, pt: (layer, pt[b, g * pages + t], 0, 0))

    grid_spec = pltpu.PrefetchScalarGridSpec(
        num_scalar_prefetch=1,
        grid=(n_seq, n_pages // pages),
        in_specs=[pl.BlockSpec((None, cdim, rows), lambda b, g, pt: (b, 0, 0)),
                  pl.BlockSpec((None, rdim, rows), lambda b, g, pt: (b, 0, 0)),
                  pl.BlockSpec((n_new, cdim), lambda b, g, pt: (b, 0)),
                  pl.BlockSpec((n_new, rdim), lambda b, g, pt: (b, 0))]
        + [c_spec(t) for t in range(pages)] + [kr_spec(t) for t in range(pages)],
        out_specs=pl.BlockSpec((rows, cdim), lambda b, g, pt: (b, 0)),
        scratch_shapes=[pltpu.VMEM((1, rows), F32), pltpu.VMEM((1, rows), F32), pltpu.VMEM((cdim, rows), F32)],
    )
    return pl.pallas_call(
        functools.partial(_mla_sample_kernel, pages=pages, heads=heads),
        grid_spec=grid_spec,
        out_shape=jax.ShapeDtypeStruct((n_seq * rows, cdim), BF16),
        compiler_params=_cparams("parallel", "arbitrary"),
        name="mla_sample",
    )(page_table, qlt, qrt, c_new, kr_new, *([cache_c] * pages), *([cache_krt] * pages))


TM = 1024


def _hgrn_layer(x, norm_g, w_in, w_out, layer, lb, gnorm, s0, *, n_prompt, prompt_len, n_sample, sample_len):
    d = x.shape[1]
    dk = d // HG_HEADS
    p = norm_matmul(x, norm_g, w_in, layer, tm=TM, tn=512)
    kw = dict(mode="hgrn", heads=HG_HEADS, dk=dk, dv=dk, scale=dk ** -0.5)
    op, sp = lin_attn(p, (lb,), gnorm, None, n_seq=n_prompt, seq_len=prompt_len, row0=0, **kw)
    os_, ss = lin_attn(p, (lb,), gnorm, s0, n_seq=n_sample, seq_len=sample_len, row0=n_prompt * prompt_len, **kw)
    o = jnp.concatenate([op, os_], axis=0)
    return matmul_res(o, w_out, layer, x, tm=TM, tn=1024, tk=512), sp, ss


def _gla_layer(x, norm_g, w_in, w_gate_up, b_gate, gnorm, w_out, layer, s0, *, n_prompt, prompt_len, n_sample,
               sample_len):
    d = x.shape[1]
    dk, dv = d // 2 // GLA_HEADS, d // GLA_HEADS
    w_pad = jnp.pad(w_in, ((0, 0), (0, LANES - GLA_GATE_RANK))).astype(BF16)[None]
    p = norm_matmul(x, norm_g, w_pad, 0, tm=TM, tn=w_pad.shape[2] // 7)
    wg = jnp.pad(w_gate_up, ((0, LANES - GLA_GATE_RANK), (0, 0))).astype(BF16)
    kw = dict(mode="gla", heads=GLA_HEADS, dk=dk, dv=dv, scale=dk ** -0.5)
    op, sp = lin_attn(p, (wg, b_gate), gnorm, None, n_seq=n_prompt, seq_len=prompt_len, row0=0, **kw)
    os_, ss = lin_attn(p, (wg, b_gate), gnorm, s0, n_seq=n_sample, seq_len=sample_len,
                       row0=n_prompt * prompt_len, **kw)
    o = jnp.concatenate([op, os_], axis=0)
    return matmul_res(o, w_out, layer, x, tm=TM, tn=1024, tk=512), sp, ss


def _moba_layer(x, norm_g, w_in, w_out, layer, cache_k, cache_v, page_table, *, n_prompt, prompt_len, n_sample,
                sample_len):
    dh, h, hkv = MB_HEAD_DIM, MB_HEADS, MB_KV_HEADS
    tp = n_prompt * prompt_len
    p = norm_matmul(x, norm_g, w_in, layer, tm=TM, tn=512)
    k = p[:, h * dh:(h + hkv) * dh]
    v = p[:, (h + hkv) * dh:]
    ap = moba_prompt(p, n_seq=n_prompt, seq_len=prompt_len, heads=h, kv_heads=hkv, dh=dh)
    qt = (p[tp:, :h * dh] * dh ** -0.5).reshape(n_sample, sample_len, h, dh).transpose(0, 3, 2, 1)
    qt = qt.reshape(n_sample, dh, h * sample_len)
    kv_of_col = jnp.arange(h * sample_len) // (h // hkv * sample_len)
    qbd = jnp.where(kv_of_col[None, None, None, :] == jnp.arange(hkv)[None, :, None, None], qt[:, None], 0.0)
    qbd = qbd.reshape(n_sample, hkv * dh, h * sample_len).astype(BF16)
    ks = k[tp:].reshape(n_sample, sample_len * hkv, dh)
    vs = v[tp:].reshape(n_sample, sample_len * hkv, dh)
    n_layers, n_pool = cache_k.shape[:2]
    ck = cache_k.reshape(n_layers, n_pool, PAGE_SIZE * hkv, dh)
    cv = cache_v.reshape(n_layers, n_pool, PAGE_SIZE * hkv, dh)
    as_ = moba_sample(qbd, ks, vs, ck, cv, page_table, layer, kv_heads=hkv, dh=dh)
    as_ = as_.reshape(n_sample, h, sample_len, dh).transpose(0, 2, 1, 3).reshape(n_sample * sample_len, h * dh)
    a = jnp.concatenate([ap, as_.astype(BF16)], axis=0)
    x = matmul_res(a, w_out, layer, x, tm=TM, tn=1024, tk=512)
    shape_p = (n_prompt, prompt_len, hkv, dh)
    shape_s = (n_sample, sample_len, hkv, dh)
    return x, k[:tp].reshape(shape_p), v[:tp].reshape(shape_p), k[tp:].reshape(shape_s), v[tp:].reshape(shape_s)


def _rope_tables(pos):
    half = MLA_ROPE // 2
    inv = ROPE_THETA ** (-jnp.arange(half, dtype=F32) / half)
    ang = pos.astype(F32)[:, None] * inv[None, :]
    reps = LANES // half
    return jnp.tile(jnp.cos(ang), (1, reps)), jnp.tile(jnp.sin(ang), (1, reps))


def _mla_layer(x, norm_g, w_in, q_norm, w_q_up, kv_norm, w_kv_up, w_out, layer, cache_c, cache_kr, page_table, pos, *,
               n_prompt, prompt_len, n_sample, sample_len):
    t = x.shape[0]
    h, half = MLA_HEADS, MLA_ROPE // 2
    tp = n_prompt * prompt_len
    d = w_in.shape[0]
    scale = (MLA_NOPE + MLA_ROPE) ** -0.5
    qa_w, ckv_w, kr_w = w_in[:, :MLA_Q_RANK], w_in[:, MLA_Q_RANK:MLA_Q_RANK + MLA_KV_RANK], w_in[:, MLA_Q_RANK + MLA_KV_RANK:]
    qpad = 1024 - MLA_Q_RANK
    zeros = lambda n: jnp.zeros((d, n), w_in.dtype)
    w_p = jnp.concatenate([qa_w, zeros(qpad), ckv_w, kr_w[:, :half], zeros(LANES - half), kr_w[:, half:],
                           zeros(LANES - half)], axis=1).astype(BF16)[None]
    p = norm_matmul(x, norm_g, w_p, 0, tm=TM, tn=256)
    w4 = w_q_up.reshape(MLA_Q_RANK, h, MLA_NOPE + MLA_ROPE)
    w_q = jnp.concatenate([w4[:, :, :MLA_NOPE].reshape(MLA_Q_RANK, h * MLA_NOPE),
                           w4[:, :, MLA_NOPE:MLA_NOPE + half].reshape(MLA_Q_RANK, h * half),
                           w4[:, :, MLA_NOPE + half:].reshape(MLA_Q_RANK, h * half)], axis=1).astype(BF16)[None]
    q3 = norm_matmul(p, q_norm, w_q, 0, tm=TM, tn=512)
    c32 = rmsnorm_rows(p, kv_norm, col=1024 // MLA_KV_RANK, tm=TM)
    cos, sin = _rope_tables(pos)
    rc = h * MLA_NOPE // (h * half)
    qr1, qr2 = rope_pairs(q3, cos, sin, col1=rc, col2=rc + 1, width=h * half, tm=TM)
    kc = (1024 + MLA_KV_RANK) // LANES
    kr1, kr2 = rope_pairs(p, cos, sin, col1=kc, col2=kc + 1, width=LANES, tm=TM)
    kr = jnp.concatenate([kr1[:, :half], kr2[:, :half]], axis=1)
    q_rope = (jnp.concatenate([qr1.reshape(t, h, half), qr2.reshape(t, h, half)], axis=-1) * scale).astype(BF16)
    wkv = w_kv_up.reshape(MLA_KV_RANK, h, MLA_NOPE + MLA_V)
    w_uk = wkv[:, :, :MLA_NOPE] * scale
    w_uv = wkv[:, :, MLA_NOPE:].transpose(1, 0, 2).astype(BF16)
    c16, kr16 = c32.astype(BF16), kr.astype(BF16)
    tq, tk = MLA_TQ, MLA_TK
    qrt_p = q_rope[:tp].reshape(tp // tq, tq, h, MLA_ROPE).transpose(0, 3, 2, 1).reshape(tp // tq, MLA_ROPE, h * tq)
    ct = c16[:tp].reshape(n_prompt, prompt_len // tk, tk, MLA_KV_RANK).transpose(0, 1, 3, 2)
    op = mla_prompt(q3, qrt_p, c16, ct, kr16, w_uk.transpose(1, 0, 2).astype(BF16), w_uv,
                    n_seq=n_prompt, seq_len=prompt_len)
    ql_s = head_matmul(q3[tp:], w_uk.transpose(1, 2, 0).astype(BF16), tm=n_sample * sample_len)
    rows = sample_len * h
    qlt_s = ql_s.reshape(n_sample, rows, MLA_KV_RANK).transpose(0, 2, 1)
    qrt_s = q_rope[tp:].reshape(n_sample, rows, MLA_ROPE).transpose(0, 2, 1)
    os_ = mla_sample(qlt_s, qrt_s, c32[tp:], kr[tp:], cache_c, jnp.swapaxes(cache_kr, 2, 3), page_table, layer, heads=h)
    os_ = head_matmul(os_.reshape(n_sample * sample_len, h * MLA_KV_RANK), w_uv, tm=n_sample * sample_len)
    o = jnp.concatenate([op, os_], axis=0)
    x = matmul_res(o, w_out, layer, x, tm=TM, tn=1024, tk=512)
    return (x, c32[:tp].reshape(n_prompt, prompt_len, -1), kr[:tp].reshape(n_prompt, prompt_len, -1),
            c32[tp:].reshape(n_sample, sample_len, -1), kr[tp:].reshape(n_sample, sample_len, -1))


def _dense_ffn(x, norm_g, w_in, w_out, layer):
    act = swiglu_proj(x, norm_g, w_in[:, None], layer, None, tm=TM, tn=512)
    return matmul_res(act, w_out, layer, x, tm=TM, tn=1024, tk=512)


def _moe_ffn(x, norm_g, wr, w_in, w_out, layer):
    rw, h = router(x, norm_g, wr, tm=512)
    act = swiglu_proj(h, None, w_in, layer, rw, tm=TM, tn=256)
    n_layers, e, f, d = w_out.shape
    return matmul_res(act, w_out.reshape(n_layers, e * f, d), layer, x, tm=TM, tn=1024, tk=512)


def kernel(x_prompt, x_sample, state_hgrn, state_gla, cache_moba_k, cache_moba_v, cache_mla_latent, cache_mla_krope,
           page_table, norm_mixer, norm_ffn, norm_final, hgrn_w_in, hgrn_lb_logits, hgrn_gnorm, hgrn_w_out, gla_w_in,
           gla_w_gate_up, gla_b_gate, gla_gnorm, gla_w_out, moba_w_in, moba_w_out, mla_w_in, mla_q_norm, mla_w_q_up,
           mla_kv_norm, mla_w_kv_up, mla_w_out, ffn_w_in, ffn_w_out, moe_router, moe_w_in, moe_w_out):
    n_prompt, prompt_len, d = x_prompt.shape
    n_sample, sample_len, _ = x_sample.shape
    past_len = page_table.shape[1] * PAGE_SIZE
    depth = norm_mixer.shape[0]
    dims = dict(n_prompt=n_prompt, prompt_len=prompt_len, n_sample=n_sample, sample_len=sample_len)
    tp = n_prompt * prompt_len
    pos = jnp.concatenate([jnp.tile(jnp.arange(prompt_len, dtype=jnp.int32), n_prompt),
                           jnp.tile(past_len + jnp.arange(sample_len, dtype=jnp.int32), n_sample)])
    lb_all = jnp.cumsum(jax.nn.softmax(hgrn_lb_logits.astype(F32), axis=0), axis=0)
    x = jnp.concatenate([x_prompt.reshape(tp, d), x_sample.reshape(-1, d)], axis=0)
    bf = lambda w: w.astype(BF16)
    hgrn_w_in, hgrn_w_out, gla_w_out, moba_w_in, moba_w_out, mla_w_out = (
        bf(hgrn_w_in), bf(hgrn_w_out), bf(gla_w_out), bf(moba_w_in), bf(moba_w_out), bf(mla_w_out))
    ffn_w_in, ffn_w_out, moe_w_in, moe_w_out = bf(ffn_w_in), bf(ffn_w_out), bf(moe_w_in), bf(moe_w_out)
    outs = {k: [] for k in ("hg_p", "hg_s", "gla_p", "gla_s", "mbk_p", "mbv_p", "mbk_s", "mbv_s",
                            "mlc_p", "mlr_p", "mlc_s", "mlr_s")}
    for i in range(depth):
        m, j = i % 4, i // 4
        if m == 0:
            x, sp, ss = _hgrn_layer(x, norm_mixer[i], hgrn_w_in, hgrn_w_out, j, lb_all[i], hgrn_gnorm[j],
                                    state_hgrn[j], **dims)
            outs["hg_p"].append(sp)
            outs["hg_s"].append(ss)
        elif m == 1:
            x, sp, ss = _gla_layer(x, norm_mixer[i], gla_w_in[j], gla_w_gate_up[j], gla_b_gate[j], gla_gnorm[j],
                                   gla_w_out, j, state_gla[j], **dims)
            outs["gla_p"].append(sp)
            outs["gla_s"].append(ss)
        elif m == 2:
            x, kp, vp, ks, vs = _moba_layer(x, norm_mixer[i], moba_w_in, moba_w_out, j, cache_moba_k, cache_moba_v,
                                            page_table, **dims)
            for key, val in zip(("mbk_p", "mbv_p", "mbk_s", "mbv_s"), (kp, vp, ks, vs)):
                outs[key].append(val)
        else:
            x, cp, rp, cs, rs = _mla_layer(x, norm_mixer[i], mla_w_in[j], mla_q_norm[j], mla_w_q_up[j], mla_kv_norm[j],
                                           mla_w_kv_up[j], mla_w_out, j, cache_mla_latent, cache_mla_krope, page_table,
                                           pos, **dims)
            for key, val in zip(("mlc_p", "mlr_p", "mlc_s", "mlr_s"), (cp, rp, cs, rs)):
                outs[key].append(val)
        if i % 2 == 0:
            x = _dense_ffn(x, norm_ffn[i], ffn_w_in, ffn_w_out, i // 2)
        else:
            x = _moe_ffn(x, norm_ffn[i], moe_router[i // 2], moe_w_in, moe_w_out, i // 2)
    y = rmsnorm_rows(x, norm_final, tm=TM)
    return (y[:tp].reshape(x_prompt.shape), y[tp:].reshape(x_sample.shape),
            *(jnp.stack(outs[k]) for k in ("hg_p", "hg_s", "gla_p", "gla_s", "mbk_p", "mbv_p", "mbk_s", "mbv_s",
                                           "mlc_p", "mlr_p", "mlc_s", "mlr_s")))
```

```python
import functools

import jax
import jax.numpy as jnp
from jax import lax
from jax.experimental import pallas as pl
from jax.experimental.pallas import tpu as pltpu

F32 = jnp.float32
BF16 = jnp.bfloat16
NEG_INF = float("-inf")

EPS = 1e-6
PAGE_SIZE = 128
HG_HEADS = 16
GLA_HEADS = 4
GLA_GATE_RANK = 16
GLA_GATE_NORM = 16.0
MB_HEADS = 16
MB_KV_HEADS = 4
MB_HEAD_DIM = 128
MB_BLOCK = 256
MB_TOPK = 3
MLA_HEADS = 16
MLA_Q_RANK = 768
MLA_KV_RANK = 512
MLA_NOPE = 128
MLA_ROPE = 64
MLA_V = 128
ROPE_THETA = 10000.0

LANES = 128
SUBLANES = 8
VMEM_LIMIT = 48 * 2 ** 20
LA_CHUNK = 64
PAGES_PER_STEP = 16
STATE_BLOCK_BYTES = 4 * 2 ** 20
MLA_TQ = 128
MLA_TK = 512
MLA_GROUP_LANES = 1024
MLA_DECODE_CHAINS = 2
MOE_TILE = 256
MOE_CHUNK = 1024
MOE_UP_SPLITS = 2


def _cparams(*sem):
    return pltpu.CompilerParams(dimension_semantics=sem, vmem_limit_bytes=VMEM_LIMIT)


def _dot(a, b):
    return jnp.dot(a, b, preferred_element_type=F32)


def _dot_nt(a, b):
    return lax.dot_general(a, b, (((1,), (1,)), ((), ())), preferred_element_type=F32)


def _dot_tn(a, b):
    return lax.dot_general(a, b, (((0,), (0,)), ((), ())), preferred_element_type=F32)


def _norm_rows(x, g):
    return x * lax.rsqrt(jnp.mean(x * x, axis=-1, keepdims=True) + EPS) * g


def _lane_col(x, j):
    lane = lax.broadcasted_iota(jnp.int32, x.shape, 1)
    return jnp.sum(jnp.where(lane == j, x, 0.0), axis=-1, keepdims=True)


def _round_up(n, m):
    return -(-n // m) * m


def _norm_matmul_kernel(x_ref, g_ref, w_ref, o_ref, h_ref):
    @pl.when(pl.program_id(1) == 0)
    def _():
        h_ref[...] = _norm_rows(x_ref[...], g_ref[...]).astype(BF16)

    o_ref[...] = _dot(h_ref[...], w_ref[...]).astype(o_ref.dtype)


def norm_matmul(x, g, w, layer, *, tm, tn, out_dtype=F32):
    m = x.shape[0]
    _, k, n = w.shape
    return pl.pallas_call(
        _norm_matmul_kernel,
        grid=(m // tm, n // tn),
        in_specs=[pl.BlockSpec((tm, k), lambda i, j: (i, 0)),
                  pl.BlockSpec((1, k), lambda i, j: (0, 0)),
                  pl.BlockSpec((None, k, tn), lambda i, j: (layer, 0, j))],
        out_specs=pl.BlockSpec((tm, tn), lambda i, j: (i, j)),
        out_shape=jax.ShapeDtypeStruct((m, n), out_dtype),
        scratch_shapes=[pltpu.VMEM((tm, k), BF16)],
        compiler_params=_cparams("parallel", "arbitrary"),
        name="norm_matmul",
    )(x, g.reshape(1, k).astype(F32), w)


def _swiglu_kernel(*refs, has_norm, has_scale, nf):
    refs = list(refs)
    x_ref = refs.pop(0)
    g_ref = refs.pop(0) if has_norm else None
    wg_ref, wu_ref = refs.pop(0), refs.pop(0)
    rw_ref = refs.pop(0) if has_scale else None
    o_ref = refs.pop(0)
    if has_norm:
        h_ref = refs.pop(0)

        @pl.when(pl.program_id(1) == 0)
        def _():
            h_ref[...] = _norm_rows(x_ref[...], g_ref[...]).astype(BF16)

        h = h_ref[...]
    else:
        h = x_ref[...]
    a = _dot(h, wg_ref[...])
    u = _dot(h, wu_ref[...])
    act = a * jax.nn.sigmoid(a) * u
    if has_scale:
        act = act * _lane_col(rw_ref[...], pl.program_id(1) // nf)
    o_ref[...] = act.astype(o_ref.dtype)


def swiglu_proj(x, g, w, layer, rw, *, tm, tn):
    m, k = x.shape
    _, e, _, f2 = w.shape
    f = f2 // 2
    nf = f // tn
    has_norm, has_scale = g is not None, rw is not None
    in_specs = [pl.BlockSpec((tm, k), lambda i, j: (i, 0))]
    args = [x]
    if has_norm:
        in_specs.append(pl.BlockSpec((1, k), lambda i, j: (0, 0)))
        args.append(g.reshape(1, k).astype(F32))
    in_specs += [pl.BlockSpec((None, None, k, tn), lambda i, j: (layer, j // nf, 0, j % nf)),
                 pl.BlockSpec((None, None, k, tn), lambda i, j: (layer, j // nf, 0, nf + j % nf))]
    args += [w, w]
    if has_scale:
        in_specs.append(pl.BlockSpec((tm, LANES), lambda i, j: (i, 0)))
        args.append(rw)
    return pl.pallas_call(
        functools.partial(_swiglu_kernel, has_norm=has_norm, has_scale=has_scale, nf=nf),
        grid=(m // tm, e * nf),
        in_specs=in_specs,
        out_specs=pl.BlockSpec((tm, tn), lambda i, j: (i, j)),
        out_shape=jax.ShapeDtypeStruct((m, e * f), BF16),
        scratch_shapes=[pltpu.VMEM((tm, k), BF16)] if has_norm else [],
        compiler_params=_cparams("parallel", "arbitrary"),
        name="swiglu_proj",
    )(*args)


def _matmul_res_kernel(a_ref, w_ref, r_ref, o_ref):
    @pl.when(pl.program_id(2) == 0)
    def _():
        o_ref[...] = r_ref[...]

    o_ref[...] += _dot(a_ref[...], w_ref[...])


def matmul_res(a, w, layer, res, *, tm, tn, tk):
    m, k = a.shape
    n = w.shape[2]
    return pl.pallas_call(
        _matmul_res_kernel,
        grid=(m // tm, n // tn, k // tk),
        in_specs=[pl.BlockSpec((tm, tk), lambda i, j, kk: (i, kk)),
                  pl.BlockSpec((None, tk, tn), lambda i, j, kk: (layer, kk, j)),
                  pl.BlockSpec((tm, tn), lambda i, j, kk: (i, j))],
        out_specs=pl.BlockSpec((tm, tn), lambda i, j, kk: (i, j)),
        out_shape=jax.ShapeDtypeStruct((m, n), F32),
        compiler_params=_cparams("parallel", "parallel", "arbitrary"),
        name="matmul_res",
    )(a, w, res)


def _rmsnorm_kernel(x_ref, g_ref, o_ref):
    o_ref[...] = _norm_rows(x_ref[...], g_ref[...]).astype(o_ref.dtype)


def rmsnorm_rows(x, g, *, col=0, tm, out_dtype=F32):
    m = x.shape[0]
    k = g.shape[-1]
    return pl.pallas_call(
        _rmsnorm_kernel,
        grid=(m // tm,),
        in_specs=[pl.BlockSpec((tm, k), lambda i: (i, col)),
                  pl.BlockSpec((1, k), lambda i: (0, 0))],
        out_specs=pl.BlockSpec((tm, k), lambda i: (i, 0)),
        out_shape=jax.ShapeDtypeStruct((m, k), out_dtype),
        compiler_params=_cparams("parallel"),
        name="rmsnorm_rows",
    )(x, g.reshape(1, k).astype(F32))


def _router_kernel(x_ref, g_ref, wr_ref, rw_ref, h_ref, *, n_experts):
    h = _norm_rows(x_ref[...], g_ref[...])
    h_ref[...] = h.astype(BF16)
    logits = jnp.dot(h, wr_ref[...], preferred_element_type=F32, precision=lax.Precision.HIGHEST)
    lane = lax.broadcasted_iota(jnp.int32, logits.shape, 1)
    l1 = jnp.where(lane < n_experts, logits, NEG_INF)
    m1 = jnp.max(l1, axis=-1, keepdims=True)
    i1 = jnp.min(jnp.where(l1 == m1, lane, LANES), axis=-1, keepdims=True)
    l2 = jnp.where(lane == i1, NEG_INF, l1)
    m2 = jnp.max(l2, axis=-1, keepdims=True)
    i2 = jnp.min(jnp.where(l2 == m2, lane, LANES), axis=-1, keepdims=True)
    e2 = jnp.exp(m2 - m1)
    den = 1.0 + e2
    rw_ref[...] = jnp.where(lane == i1, 1.0 / den, 0.0) + jnp.where(lane == i2, e2 / den, 0.0)


def router(x, g, wr, *, tm):
    m, k = x.shape
    n_experts = wr.shape[1]
    wr_pad = jnp.pad(wr.astype(F32), ((0, 0), (0, LANES - n_experts)))
    return pl.pallas_call(
        functools.partial(_router_kernel, n_experts=n_experts),
        grid=(m // tm,),
        in_specs=[pl.BlockSpec((tm, k), lambda i: (i, 0)),
                  pl.BlockSpec((1, k), lambda i: (0, 0)),
                  pl.BlockSpec((k, LANES), lambda i: (0, 0))],
        out_specs=[pl.BlockSpec((tm, LANES), lambda i: (i, 0)),
                   pl.BlockSpec((tm, k), lambda i: (i, 0))],
        out_shape=[jax.ShapeDtypeStruct((m, LANES), F32), jax.ShapeDtypeStruct((m, k), BF16)],
        compiler_params=_cparams("parallel"),
        name="router",
    )(x, g.reshape(1, k).astype(F32), wr_pad)


def _moe_plan(rw, n_experts):
    t = rw.shape[0]
    tile, chunk = MOE_TILE, MOE_CHUNK
    n_tiles = 2 * t // tile + n_experts
    n_rows = n_tiles * tile
    w = rw[:, :n_experts]
    mask = w > 0
    cnt = jnp.sum(mask, axis=0, dtype=jnp.int32)
    pos = jnp.cumsum(mask, axis=0, dtype=jnp.int32) - 1
    tiles_e = (cnt + tile - 1) // tile
    ends = jnp.cumsum(tiles_e)
    dest = jnp.where(mask, ((ends - tiles_e) * tile)[None, :] + pos, -1)
    tile_ids = jnp.arange(n_tiles, dtype=jnp.int32)
    tile_expert = jnp.minimum(jnp.sum(ends[None, :] <= tile_ids[:, None], axis=1), n_experts - 1).astype(jnp.int32)
    dc = dest.reshape(t // chunk, chunk, n_experts)
    c_hi_tile = jnp.max(dc, axis=1) // tile
    c_lo_tile = jnp.min(jnp.where(dc >= 0, dc, n_rows), axis=1) // tile
    lo_sel = jnp.take(c_lo_tile.T, tile_expert, axis=0)
    hi_sel = jnp.take(c_hi_tile.T, tile_expert, axis=0)
    covers = (hi_sel >= 0) & (lo_sel <= tile_ids[:, None]) & (tile_ids[:, None] <= hi_sel)
    cidx = jnp.arange(t // chunk, dtype=jnp.int32)[None, :]
    first = jnp.min(jnp.where(covers, cidx, t // chunk), axis=1)
    last = jnp.max(jnp.where(covers, cidx, -1), axis=1)
    c_lo = jnp.where(last >= 0, first, 0).astype(jnp.int32)
    n_c = jnp.where(last >= 0, last - first + 1, 0).astype(jnp.int32)
    d3 = dest.reshape(t // tile, tile, n_experts)
    lo = jnp.min(jnp.where(d3 >= 0, d3, n_rows), axis=1)
    hi = jnp.max(d3, axis=1)
    b0 = jnp.where(hi >= 0, lo // tile, 0).astype(jnp.int32).reshape(-1)
    nb = jnp.where(hi >= 0, hi // tile - lo // tile + 1, 0).astype(jnp.int32).reshape(-1)
    dest_f = dest.astype(F32)
    pad = _round_up(n_experts, SUBLANES) - n_experts
    return dict(n_tiles=n_tiles, tile_expert=tile_expert, c_lo=c_lo, n_c=n_c, b0=b0, nb=nb,
                dest=jnp.pad(dest_f, ((0, 0), (0, LANES - n_experts)), constant_values=-1.0),
                dest_t=jnp.pad(dest_f.T, ((0, pad), (0, 0)), constant_values=-1.0),
                w_t=jnp.pad(w.T, ((0, pad), (0, 0))))


def _moe_gather_kernel(te_ref, clo_ref, nc_ref, dest_ref, w_ref, h_ref, o_ref, rw_ref, acc_scr, rw_scr):
    r, c = pl.program_id(0), pl.program_id(1)

    @pl.when(c == 0)
    def _():
        acc_scr[...] = jnp.zeros_like(acc_scr)
        rw_scr[...] = jnp.zeros_like(rw_scr)

    @pl.when(c < nc_ref[r])
    def _():
        tile = o_ref.shape[0]
        e = te_ref[r]
        rel = dest_ref[pl.ds(e, 1), :] - (r * tile).astype(F32)
        row = lax.broadcasted_iota(jnp.int32, (tile, rel.shape[1]), 0).astype(F32)
        hit = rel == row
        acc_scr[...] += _dot(jnp.where(hit, 1.0, 0.0).astype(BF16), h_ref[...])
        rw_scr[...] += jnp.sum(jnp.where(hit, w_ref[pl.ds(e, 1), :], 0.0), axis=-1, keepdims=True)

    @pl.when(c == pl.num_programs(1) - 1)
    def _():
        o_ref[...] = acc_scr[...].astype(o_ref.dtype)
        rw_ref[...] = rw_scr[...]


def moe_gather(h, plan):
    t, k = h.shape
    tile, chunk = MOE_TILE, MOE_CHUNK
    n_tiles = plan["n_tiles"]
    ne = plan["dest_t"].shape[0]

    def chunk_of(r, c, te, clo, nc):
        return clo[r] + jnp.minimum(c, jnp.maximum(nc[r], 1) - 1)

    grid_spec = pltpu.PrefetchScalarGridSpec(
        num_scalar_prefetch=3,
        grid=(n_tiles, t // chunk),
        in_specs=[pl.BlockSpec((ne, chunk), lambda *a: (0, chunk_of(*a))),
                  pl.BlockSpec((ne, chunk), lambda *a: (0, chunk_of(*a))),
                  pl.BlockSpec((chunk, k), lambda *a: (chunk_of(*a), 0))],
        out_specs=[pl.BlockSpec((tile, k), lambda r, c, te, clo, nc: (r, 0)),
                   pl.BlockSpec((tile, 1), lambda r, c, te, clo, nc: (r, 0))],
        scratch_shapes=[pltpu.VMEM((tile, k), F32), pltpu.VMEM((tile, 1), F32)],
    )
    return pl.pallas_call(
        _moe_gather_kernel,
        grid_spec=grid_spec,
        out_shape=[jax.ShapeDtypeStruct((n_tiles * tile, k), BF16), jax.ShapeDtypeStruct((n_tiles * tile, 1), F32)],
        compiler_params=_cparams("parallel", "arbitrary"),
        name="moe_gather",
    )(plan["tile_expert"], plan["c_lo"], plan["n_c"], plan["dest_t"], plan["w_t"], h)


def _moe_up_kernel(te_ref, x_ref, wg_ref, wu_ref, rw_ref, o_ref):
    del te_ref
    h = x_ref[...]
    a = _dot(h, wg_ref[...])
    u = _dot(h, wu_ref[...])
    o_ref[...] = (a * jax.nn.sigmoid(a) * u * rw_ref[...]).astype(o_ref.dtype)


def moe_up(xs, row_w, w, layer, plan):
    n_rows, k = xs.shape
    f = w.shape[3] // 2
    tile = MOE_TILE
    nf = MOE_UP_SPLITS
    tn = f // nf
    grid_spec = pltpu.PrefetchScalarGridSpec(
        num_scalar_prefetch=1,
        grid=(nf, n_rows // tile),
        in_specs=[pl.BlockSpec((tile, k), lambda j, r, te: (r, 0)),
                  pl.BlockSpec((None, None, k, tn), lambda j, r, te: (layer, te[r], 0, j)),
                  pl.BlockSpec((None, None, k, tn), lambda j, r, te: (layer, te[r], 0, nf + j)),
                  pl.BlockSpec((tile, 1), lambda j, r, te: (r, 0))],
        out_specs=pl.BlockSpec((tile, tn), lambda j, r, te: (r, j)),
    )
    return pl.pallas_call(
        _moe_up_kernel,
        grid_spec=grid_spec,
        out_shape=jax.ShapeDtypeStruct((n_rows, f), BF16),
        compiler_params=_cparams("arbitrary", "arbitrary"),
        name="moe_up",
    )(plan["tile_expert"], xs, w, w, row_w)


def _moe_down_kernel(te_ref, a_ref, w_ref, o_ref):
    del te_ref
    o_ref[...] = _dot(a_ref[...], w_ref[...]).astype(o_ref.dtype)


def moe_down(act, w, layer, plan):
    n_rows, f = act.shape
    d = w.shape[3]
    tile = MOE_TILE
    grid_spec = pltpu.PrefetchScalarGridSpec(
        num_scalar_prefetch=1,
        grid=(n_rows // tile,),
        in_specs=[pl.BlockSpec((tile, f), lambda r, te: (r, 0)),
                  pl.BlockSpec((None, None, f, d), lambda r, te: (layer, te[r], 0, 0))],
        out_specs=pl.BlockSpec((tile, d), lambda r, te: (r, 0)),
    )
    return pl.pallas_call(
        _moe_down_kernel,
        grid_spec=grid_spec,
        out_shape=jax.ShapeDtypeStruct((n_rows, d), BF16),
        compiler_params=_cparams("arbitrary"),
        name="moe_down",
    )(plan["tile_expert"], act, w)


def _moe_combine_kernel(b0_ref, nb_ref, dest_ref, x_ref, y_ref, o_ref, *, n_experts):
    tt, e, kk = pl.program_id(0), pl.program_id(1), pl.program_id(2)
    idx = tt * n_experts + e

    @pl.when((e == 0) & (kk == 0))
    def _():
        o_ref[...] = x_ref[...]

    @pl.when(kk < nb_ref[idx])
    def _():
        tile = y_ref.shape[0]
        rel = _lane_col(dest_ref[...], e) - ((b0_ref[idx] + kk) * tile).astype(F32)
        lane = lax.broadcasted_iota(jnp.int32, (tile, tile), 1).astype(F32)
        onehot = jnp.where(rel == lane, 1.0, 0.0).astype(BF16)
        o_ref[...] += _dot(onehot, y_ref[...])


def moe_combine(x, y, plan, n_experts):
    t, d = x.shape
    tile = MOE_TILE

    def y_map(tt, e, kk, b0, nb):
        idx = tt * n_experts + e
        return (b0[idx] + jnp.minimum(kk, jnp.maximum(nb[idx], 1) - 1), 0)

    grid_spec = pltpu.PrefetchScalarGridSpec(
        num_scalar_prefetch=2,
        grid=(t // tile, n_experts, 2),
        in_specs=[pl.BlockSpec((tile, LANES), lambda tt, e, kk, b0, nb: (tt, 0)),
                  pl.BlockSpec((tile, d), lambda tt, e, kk, b0, nb: (tt, 0)),
                  pl.BlockSpec((tile, d), y_map)],
        out_specs=pl.BlockSpec((tile, d), lambda tt, e, kk, b0, nb: (tt, 0)),
    )
    return pl.pallas_call(
        functools.partial(_moe_combine_kernel, n_experts=n_experts),
        grid_spec=grid_spec,
        out_shape=jax.ShapeDtypeStruct((t, d), F32),
        compiler_params=_cparams("parallel", "arbitrary", "arbitrary"),
        name="moe_combine",
    )(plan["b0"], plan["nb"], plan["dest"], x, y)


def _head_matmul_kernel(a_ref, w_ref, o_ref):
    o_ref[...] = _dot(a_ref[...].astype(BF16), w_ref[...]).astype(o_ref.dtype)


def head_matmul(a, w, *, tm, out_dtype=BF16):
    m = a.shape[0]
    h, k, n = w.shape
    return pl.pallas_call(
        _head_matmul_kernel,
        grid=(m // tm, h),
        in_specs=[pl.BlockSpec((tm, k), lambda i, j: (i, j)),
                  pl.BlockSpec((None, k, n), lambda i, j: (j, 0, 0))],
        out_specs=pl.BlockSpec((tm, n), lambda i, j: (i, j)),
        out_shape=jax.ShapeDtypeStruct((m, h * n), out_dtype),
        compiler_params=_cparams("parallel", "arbitrary"),
        name="head_matmul",
    )(a, w)


def _rope_kernel(x1_ref, x2_ref, cos_ref, sin_ref, r1_ref, r2_ref, *, reps):
    cos, sin = cos_ref[...], sin_ref[...]
    if reps > 1:
        cos = jnp.concatenate([cos] * reps, axis=-1)
        sin = jnp.concatenate([sin] * reps, axis=-1)
    x1, x2 = x1_ref[...], x2_ref[...]
    r1_ref[...] = (x1 * cos - x2 * sin).astype(r1_ref.dtype)
    r2_ref[...] = (x1 * sin + x2 * cos).astype(r2_ref.dtype)


def rope_pairs(x, cos, sin, *, col1, col2, width, tm):
    m = x.shape[0]
    return pl.pallas_call(
        functools.partial(_rope_kernel, reps=width // LANES),
        grid=(m // tm,),
        in_specs=[pl.BlockSpec((tm, width), lambda i: (i, col1)),
                  pl.BlockSpec((tm, width), lambda i: (i, col2)),
                  pl.BlockSpec((tm, LANES), lambda i: (i, 0)),
                  pl.BlockSpec((tm, LANES), lambda i: (i, 0))],
        out_specs=[pl.BlockSpec((tm, width), lambda i: (i, 0))] * 2,
        out_shape=[jax.ShapeDtypeStruct((m, width), F32)] * 2,
        compiler_params=_cparams("parallel"),
        name="rope_pairs",
    )(x, x, cos, sin)


def _cumsum_rows(x, c):
    row = lax.broadcasted_iota(jnp.int32, x.shape, 0)
    sh = 1
    while sh < c:
        x = x + jnp.where(row >= sh, pltpu.roll(x, sh, 0), 0.0)
        sh *= 2
    return x


def _gla_chunk(q, k, v, lf, s_prev, c):
    dk = q.shape[1]
    b = _cumsum_rows(lf, c)
    rid = lax.broadcasted_iota(jnp.int32, (c, 1), 0) % SUBLANES
    o = jnp.sum(q * k, axis=-1, keepdims=True) * v
    for d in range(1, SUBLANES):
        kd, bd, vd = pltpu.roll(k, d, 0), pltpu.roll(b, d, 0), pltpu.roll(v, d, 0)
        w = jnp.sum(q * kd * jnp.exp(jnp.minimum(b - bd, 0.0)), axis=-1, keepdims=True)
        o = o + jnp.where(rid >= d, w, 0.0) * vd
    if c > SUBLANES:
        t = lax.broadcasted_iota(jnp.int32, (c, c), 0)
        s = lax.broadcasted_iota(jnp.int32, (c, c), 1)
        a = jnp.zeros((c, c), F32)
        blk = 2 * SUBLANES
        while blk <= c:
            half = blk // 2
            ref = jnp.concatenate(
                [jnp.broadcast_to(b[i * blk + half - 1:i * blk + half, :], (blk, dk)) for i in range(c // blk)], axis=0)
            qt = (q * jnp.exp(jnp.minimum(b - ref, 0.0))).astype(BF16)
            kt = (k * jnp.exp(jnp.minimum(ref - b, 0.0))).astype(BF16)
            pair = (t // blk == s // blk) & (t % blk >= half) & (s % blk < half)
            a = a + jnp.where(pair, _dot_nt(qt, kt), 0.0)
            blk *= 2
        o = o + _dot(a.astype(BF16), v.astype(BF16))
    o = o + _dot((q * jnp.exp(b)).astype(BF16), s_prev.astype(BF16))
    b_end = b[c - 1:c, :]
    kt = (k * jnp.exp(b_end - b)).astype(BF16)
    decay = jnp.exp(jnp.sum(lf.T, axis=-1, keepdims=True))
    s_new = decay * s_prev + _dot_tn(kt, v.astype(BF16))
    return o, s_new


def _lin_attn_kernel(*refs, mode, sample, c, n_inner, scale):
    refs = list(refs)
    if mode == "hgrn":
        q_ref, f_ref, v_ref, g_ref, lb_ref, gn_ref = refs[:6]
        refs = refs[6:]
    else:
        q_ref, k_ref, v_ref, g_ref, gl_ref, wg_ref, bg_ref, gn_ref = refs[:8]
        refs = refs[8:]
    if sample:
        s0_ref, o_ref, so_ref = refs
    else:
        o_ref, so_ref, s_scr = refs

    def prep(rows):
        if mode == "hgrn":
            qr, fr, lb = q_ref[rows, :], f_ref[rows, :], lb_ref[...]
            q = qr * jax.nn.sigmoid(qr) * scale
            lf = jnp.log(lb + (1.0 - lb) * jax.nn.sigmoid(fr))
            k = (1.0 - lb) * jax.nn.sigmoid(-fr)
        else:
            q = q_ref[rows, :] * scale
            k = k_ref[rows, :]
            z = _dot(gl_ref[rows, :].astype(BF16), wg_ref[...]) + bg_ref[...]
            lf = -(jnp.maximum(-z, 0.0) + jnp.log1p(jnp.exp(-jnp.abs(z)))) * (1.0 / GLA_GATE_NORM)
        return q, k, v_ref[rows, :], lf

    def finish(rows, o):
        g = g_ref[rows, :]
        o_ref[rows, :] = (_norm_rows(o, gn_ref[...]) * (g * jax.nn.sigmoid(g))).astype(o_ref.dtype)

    if sample:
        def body(si, carry):
            rows = pl.ds(pl.multiple_of(si * c, c), c)
            q, k, v, lf = prep(rows)
            o, s_new = _gla_chunk(q, k, v, lf, s0_ref[si], c)
            so_ref[si] = s_new
            finish(rows, o)
            return carry
    else:
        @pl.when(pl.program_id(2) == 0)
        def _():
            s_scr[...] = jnp.zeros_like(s_scr)

        def body(ci, carry):
            rows = pl.ds(pl.multiple_of(ci * c, c), c)
            q, k, v, lf = prep(rows)
            o, s_new = _gla_chunk(q, k, v, lf, s_scr[...], c)
            s_scr[...] = s_new
            finish(rows, o)
            return carry

    lax.fori_loop(0, n_inner, body, 0, unroll=2)

    if not sample:
        @pl.when(pl.program_id(2) == pl.num_programs(2) - 1)
        def _():
            so_ref[...] = s_scr[...]


def lin_attn(p, extra, gnorm, s0, *, mode, heads, dk, dv, n_seq, seq_len, row0, scale):
    sample = s0 is not None
    if sample:
        c = seq_len
        nb = max(1, min(n_seq, STATE_BLOCK_BYTES // (dk * dv * 4)))
        tb, n_inner = nb * c, nb
        grid = (n_seq // nb, heads, 1)
        rb = row0 // tb
        row = lambda b, h, i: rb + b
    else:
        c, tb = LA_CHUNK, 4 * LA_CHUNK
        n_inner = tb // c
        nt = seq_len // tb
        grid = (n_seq, heads, nt)
        rb = row0 // tb
        row = lambda b, h, i: rb + b * nt + i
    if mode == "hgrn":
        lb, = extra
        nh = heads
        in_specs = [pl.BlockSpec((tb, dk), lambda b, h, i: (row(b, h, i), h)),
                    pl.BlockSpec((tb, dk), lambda b, h, i: (row(b, h, i), nh + h)),
                    pl.BlockSpec((tb, dv), lambda b, h, i: (row(b, h, i), 2 * nh + h)),
                    pl.BlockSpec((tb, dv), lambda b, h, i: (row(b, h, i), 3 * nh + h)),
                    pl.BlockSpec((1, dk), lambda b, h, i: (0, h)),
                    pl.BlockSpec((1, dv), lambda b, h, i: (0, 0))]
        args = [p, p, p, p, lb.reshape(1, -1).astype(F32), gnorm.reshape(1, dv).astype(F32)]
    else:
        wg, bg = extra
        nh = heads
        voff = 2 * heads * dk // dv
        gloff = (2 * heads * dk + 2 * heads * dv) // LANES
        in_specs = [pl.BlockSpec((tb, dk), lambda b, h, i: (row(b, h, i), h)),
                    pl.BlockSpec((tb, dk), lambda b, h, i: (row(b, h, i), nh + h)),
                    pl.BlockSpec((tb, dv), lambda b, h, i: (row(b, h, i), voff + h)),
                    pl.BlockSpec((tb, dv), lambda b, h, i: (row(b, h, i), voff + nh + h)),
                    pl.BlockSpec((tb, LANES), lambda b, h, i: (row(b, h, i), gloff)),
                    pl.BlockSpec((LANES, dk), lambda b, h, i: (0, h)),
                    pl.BlockSpec((1, dk), lambda b, h, i: (0, h)),
                    pl.BlockSpec((1, dv), lambda b, h, i: (0, 0))]
        args = [p, p, p, p, p, wg, bg.reshape(1, -1).astype(F32), gnorm.reshape(1, dv).astype(F32)]
    out_specs = [pl.BlockSpec((tb, dv), lambda b, h, i: (row(b, h, i) - rb, h))]
    out_shape = [jax.ShapeDtypeStruct((n_seq * seq_len, heads * dv), BF16)]
    scratch = []
    if sample:
        in_specs.append(pl.BlockSpec((nb, None, dk, dv), lambda b, h, i: (b, h, 0, 0)))
        args.append(s0)
        out_specs.append(pl.BlockSpec((nb, None, dk, dv), lambda b, h, i: (b, h, 0, 0)))
    else:
        out_specs.append(pl.BlockSpec((None, None, dk, dv), lambda b, h, i: (b, h, 0, 0)))
        scratch.append(pltpu.VMEM((dk, dv), F32))
    out_shape.append(jax.ShapeDtypeStruct((n_seq, heads, dk, dv), F32))
    return pl.pallas_call(
        functools.partial(_lin_attn_kernel, mode=mode, sample=sample, c=c, n_inner=n_inner, scale=scale),
        grid=grid,
        in_specs=in_specs,
        out_specs=out_specs,
        out_shape=out_shape,
        scratch_shapes=scratch,
        compiler_params=_cparams("parallel", "parallel", "arbitrary"),
        name=f"lin_attn_{mode}_{'sample' if sample else 'prompt'}",
    )(*args)


def _moba_weights(gate, m_all, n_prev, own, n_static):
    ridx = lax.broadcasted_iota(jnp.int32, gate.shape, 0)
    cnt = jnp.zeros(gate.shape, F32)
    for jp in range(n_static):
        row = gate[jp:jp + 1, :]
        beats = (row > gate) | ((row == gate) & (jp < ridx))
        cnt = cnt + jnp.where(beats & (jp < n_prev), 1.0, 0.0)
    sel = ((cnt < MB_TOPK) & (ridx < n_prev)) | (ridx == own)
    mv = jnp.where(sel, m_all, NEG_INF)
    return jnp.where(sel, jnp.exp(mv - jnp.max(mv, axis=0, keepdims=True)), 0.0)


def _init_partials(m_scr, l_scr, g_scr):
    m_scr[...] = jnp.full(m_scr.shape, NEG_INF, F32)
    g_scr[...] = jnp.full(g_scr.shape, NEG_INF, F32)
    l_scr[...] = jnp.zeros(l_scr.shape, F32)


def _moba_prompt_kernel(q_ref, k_ref, v_ref, o_ref, qt_scr, m_scr, l_scr, g_scr, w_scr, o_scr, *, scale, group, n_blk):
    i = pl.program_id(2)
    dh = k_ref.shape[1]
    for hh in range(group):
        qt_scr[hh] = (q_ref[:, hh * dh:(hh + 1) * dh] * scale).T.astype(BF16)
    _init_partials(m_scr, l_scr, g_scr)
    kidx = lax.broadcasted_iota(jnp.int32, (MB_BLOCK, MB_BLOCK), 0)
    qidx = lax.broadcasted_iota(jnp.int32, (MB_BLOCK, MB_BLOCK), 1)
    causal = kidx <= qidx

    def blk(j, carry):
        rows = pl.ds(pl.multiple_of(j * MB_BLOCK, MB_BLOCK), MB_BLOCK)
        kj, vj = k_ref[rows, :].astype(BF16), v_ref[rows, :].astype(BF16)
        keep = causal | (j < i)
        for hh in range(group):
            s = _dot(kj, qt_scr[hh])
            g_scr[hh, pl.ds(j, 1), :] = jnp.sum(s, axis=0, keepdims=True)
            s = jnp.where(keep, s, NEG_INF)
            m = jnp.max(s, axis=0, keepdims=True)
            p = jnp.exp(s - m)
            m_scr[hh, pl.ds(j, 1), :] = m
            l_scr[hh, pl.ds(j, 1), :] = jnp.sum(p, axis=0, keepdims=True)
            o_scr[hh, j] = _dot_tn(vj, p.astype(BF16))
        return carry

    lax.fori_loop(0, i + 1, blk, 0)
    for hh in range(group):
        w = _moba_weights(g_scr[hh], m_scr[hh], i, i, n_blk)
        den = jnp.sum(w * l_scr[hh], axis=0, keepdims=True)
        w_scr[...] = w

        def comb(j, acc):
            return acc + w_scr[pl.ds(j, 1), :] * o_scr[hh, j]

        acc = lax.fori_loop(0, i + 1, comb, jnp.zeros((dh, MB_BLOCK), F32))
        o_ref[:, hh * dh:(hh + 1) * dh] = (acc / den).T.astype(o_ref.dtype)


def moba_prompt(p, *, n_seq, seq_len, heads, kv_heads, dh):
    group = heads // kv_heads
    nq = seq_len // MB_BLOCK
    nbp = _round_up(nq, SUBLANES)
    koff = heads
    voff = heads + kv_heads
    return pl.pallas_call(
        functools.partial(_moba_prompt_kernel, scale=dh ** -0.5, group=group, n_blk=nq),
        grid=(n_seq, kv_heads, nq),
        in_specs=[pl.BlockSpec((MB_BLOCK, group * dh), lambda b, g, i: (b * nq + i, g)),
                  pl.BlockSpec((seq_len, dh), lambda b, g, i: (b, koff + g)),
                  pl.BlockSpec((seq_len, dh), lambda b, g, i: (b, voff + g))],
        out_specs=pl.BlockSpec((MB_BLOCK, group * dh), lambda b, g, i: (b * nq + i, g)),
        out_shape=jax.ShapeDtypeStruct((n_seq * seq_len, heads * dh), BF16),
        scratch_shapes=[pltpu.VMEM((group, dh, MB_BLOCK), BF16)]
        + [pltpu.VMEM((group, nbp, MB_BLOCK), F32)] * 3
        + [pltpu.VMEM((nbp, MB_BLOCK), F32), pltpu.VMEM((group, nq, dh, MB_BLOCK), F32)],
        compiler_params=_cparams("parallel", "parallel", "arbitrary"),
        name="moba_prompt",
    )(p, p, p)


def _moba_sample_kernel(pt_ref, qbd_ref, kn_ref, vn_ref, *refs, pages, kv_heads, n_new):
    del pt_ref
    k_refs, v_refs = refs[:pages], refs[pages:2 * pages]
    o_ref, m_scr, l_scr, g_scr, o_scr = refs[2 * pages:]
    g = pl.program_id(1)
    rows, dh = o_ref.shape
    rpk = rows // kv_heads
    ppb = MB_BLOCK // PAGE_SIZE
    bps = pages // ppb

    def scores(k_ref, n):
        parts = [_dot(k_ref[pl.ds(gk, n, stride=kv_heads), :].astype(BF16), qbd_ref[gk * dh:(gk + 1) * dh, :])
                 for gk in range(kv_heads)]
        return functools.reduce(jnp.add, parts)

    def values(v_ref, p, n):
        lane_g = lax.broadcasted_iota(jnp.int32, p.shape, 1) // rpk
        parts = [_dot_tn(v_ref[pl.ds(gk, n, stride=kv_heads), :].astype(BF16),
                         jnp.where(lane_g == gk, p, 0.0).astype(BF16)) for gk in range(kv_heads)]
        return functools.reduce(jnp.add, parts)

    @pl.when(g == 0)
    def _():
        _init_partials(m_scr, l_scr, g_scr)

    for bb in range(bps):
        j = g * bps + bb
        ss = [scores(k_refs[bb * ppb + t], PAGE_SIZE) for t in range(ppb)]
        g_scr[pl.ds(j, 1), :] = functools.reduce(jnp.add, [jnp.sum(s, axis=0, keepdims=True) for s in ss])
        m = functools.reduce(jnp.maximum, [jnp.max(s, axis=0, keepdims=True) for s in ss])
        ps = [jnp.exp(s - m) for s in ss]
        m_scr[pl.ds(j, 1), :] = m
        l_scr[pl.ds(j, 1), :] = functools.reduce(jnp.add, [jnp.sum(p, axis=0, keepdims=True) for p in ps])
        o_scr[j] = functools.reduce(jnp.add, [values(v_refs[bb * ppb + t], ps[t], PAGE_SIZE) for t in range(ppb)])

    @pl.when(g == pl.num_programs(1) - 1)
    def _():
        n_prev = o_scr.shape[0] - 1
        s = scores(kn_ref, n_new)
        tok = lax.broadcasted_iota(jnp.int32, s.shape, 0)
        qi = lax.broadcasted_iota(jnp.int32, s.shape, 1) % n_new
        s = jnp.where(tok <= qi, s, NEG_INF)
        m = jnp.max(s, axis=0, keepdims=True)
        p = jnp.exp(s - m)
        m_scr[n_prev:n_prev + 1, :] = m
        l_scr[n_prev:n_prev + 1, :] = jnp.sum(p, axis=0, keepdims=True)
        o_scr[n_prev] = values(vn_ref, p, n_new)
        w = _moba_weights(g_scr[...], m_scr[...], n_prev, n_prev, n_prev)
        den = jnp.sum(w * l_scr[...], axis=0, keepdims=True)
        acc = jnp.zeros((dh, rows), F32)
        for jj in range(n_prev + 1):
            acc = acc + w[jj:jj + 1, :] * o_scr[jj]
        o_ref[...] = (acc / den).T.astype(o_ref.dtype)


def moba_sample(qbd, k_new, v_new, cache_k, cache_v, page_table, layer, *, kv_heads, dh):
    n_seq, _, rows = qbd.shape
    n_new = k_new.shape[1] // kv_heads
    n_pages = page_table.shape[1]
    pages = PAGES_PER_STEP
    n_prev = n_pages * PAGE_SIZE // MB_BLOCK
    nbp = _round_up(n_prev + 1, SUBLANES)
    assert n_pages % pages == 0 and pages % (MB_BLOCK // PAGE_SIZE) == 0
    assert n_new <= MB_BLOCK

    def page_spec(t):
        return pl.BlockSpec((None, None, PAGE_SIZE * kv_heads, dh),
                            lambda b, g, pt: (layer, pt[b, g * pages + t], 0, 0))

    grid_spec = pltpu.PrefetchScalarGridSpec(
        num_scalar_prefetch=1,
        grid=(n_seq, n_pages // pages),
        in_specs=[pl.BlockSpec((None, kv_heads * dh, rows), lambda b, g, pt: (b, 0, 0)),
                  pl.BlockSpec((None, n_new * kv_heads, dh), lambda b, g, pt: (b, 0, 0)),
                  pl.BlockSpec((None, n_new * kv_heads, dh), lambda b, g, pt: (b, 0, 0))]
        + [page_spec(t) for t in range(pages)] * 2,
        out_specs=pl.BlockSpec((None, rows, dh), lambda b, g, pt: (b, 0, 0)),
        scratch_shapes=[pltpu.VMEM((nbp, rows), F32)] * 3 + [pltpu.VMEM((n_prev + 1, dh, rows), F32)],
    )
    return pl.pallas_call(
        functools.partial(_moba_sample_kernel, pages=pages, kv_heads=kv_heads, n_new=n_new),
        grid_spec=grid_spec,
        out_shape=jax.ShapeDtypeStruct((n_seq, rows, dh), F32),
        compiler_params=_cparams("parallel", "arbitrary"),
        name="moba_sample",
    )(page_table, qbd, k_new, v_new, *([cache_k] * pages), *([cache_v] * pages))


def _flash_init(m_scr, l_scr, acc_scr):
    m_scr[...] = jnp.full(m_scr.shape, NEG_INF, F32)
    l_scr[...] = jnp.zeros(l_scr.shape, F32)
    acc_scr[...] = jnp.zeros(acc_scr.shape, F32)


def _flash_update(s_list, pv, m_scr, l_scr, acc_scr):
    m_old = m_scr[...]
    m_new = functools.reduce(jnp.maximum, [jnp.max(s, axis=0, keepdims=True) for s in s_list] + [m_old])
    alpha = jnp.exp(m_old - m_new)
    ps = [jnp.exp(s - m_new) for s in s_list]
    l_scr[...] = alpha * l_scr[...] + functools.reduce(jnp.add, [jnp.sum(p, axis=0, keepdims=True) for p in ps])
    acc_scr[...] = alpha * acc_scr[...] + functools.reduce(
        jnp.add, [pv(idx, p.astype(BF16)) for idx, p in enumerate(ps)])
    m_scr[...] = m_new


def _mla_prompt_kernel(q_ref, qrt_ref, c_ref, ct_ref, kr_ref, wuk_ref, wuv_ref, o_ref, qlt_scr, m_scr, l_scr, acc_scr,
                       *, tq, tk, heads):
    i = pl.program_id(1)
    nope, vdim = wuk_ref.shape[2], wuv_ref.shape[2]
    for h in range(heads):
        qh = q_ref[:, h * nope:(h + 1) * nope].astype(BF16)
        qlt_scr[:, h * tq:(h + 1) * tq] = _dot_nt(wuk_ref[h], qh).astype(BF16)
    gw = m_scr.shape[1]
    hpg = gw // tq
    kidx = lax.broadcasted_iota(jnp.int32, (tk, gw), 0)
    qpos = i * tq + lax.broadcasted_iota(jnp.int32, (tk, gw), 1) % tq
    n_kb = ((i + 1) * tq + tk - 1) // tk
    for cg in range(heads // hpg):
        lanes = slice(cg * gw, (cg + 1) * gw)
        _flash_init(m_scr, l_scr, acc_scr)

        def body(j, carry):
            r0 = pl.multiple_of(j * tk, tk)
            s = _dot(c_ref[pl.ds(r0, tk), :], qlt_scr[:, lanes]) + _dot(kr_ref[pl.ds(r0, tk), :], qrt_ref[:, lanes])
            s = jnp.where(r0 + kidx <= qpos, s, NEG_INF)
            _flash_update([s], lambda idx, p: _dot(ct_ref[j], p), m_scr, l_scr, acc_scr)
            return carry

        lax.fori_loop(0, n_kb, body, 0)
        ot = (acc_scr[...] / l_scr[...]).astype(BF16)
        for hh in range(hpg):
            h = cg * hpg + hh
            o_ref[:, h * vdim:(h + 1) * vdim] = _dot_tn(ot[:, hh * tq:(hh + 1) * tq], wuv_ref[h]).astype(o_ref.dtype)


def mla_prompt(q, qrt, c, ct, kr, wuk, wuv, *, n_seq, seq_len):
    tq, tk = MLA_TQ, MLA_TK
    nq = seq_len // tq
    heads, cdim, nope = wuk.shape
    vdim = wuv.shape[2]
    rdim = kr.shape[1]
    return pl.pallas_call(
        functools.partial(_mla_prompt_kernel, tq=tq, tk=tk, heads=heads),
        grid=(n_seq, nq),
        in_specs=[pl.BlockSpec((tq, heads * nope), lambda b, i: (b * nq + i, 0)),
                  pl.BlockSpec((None, rdim, heads * tq), lambda b, i: (b * nq + i, 0, 0)),
                  pl.BlockSpec((seq_len, cdim), lambda b, i: (b, 0)),
                  pl.BlockSpec((None, seq_len // tk, cdim, tk), lambda b, i: (b, 0, 0, 0)),
                  pl.BlockSpec((seq_len, rdim), lambda b, i: (b, 0)),
                  pl.BlockSpec((heads, cdim, nope), lambda b, i: (0, 0, 0)),
                  pl.BlockSpec((heads, cdim, vdim), lambda b, i: (0, 0, 0))],
        out_specs=pl.BlockSpec((tq, heads * vdim), lambda b, i: (b * nq + i, 0)),
        out_shape=jax.ShapeDtypeStruct((n_seq * seq_len, heads * vdim), BF16),
        scratch_shapes=[pltpu.VMEM((cdim, heads * tq), BF16), pltpu.VMEM((1, MLA_GROUP_LANES), F32),
                        pltpu.VMEM((1, MLA_GROUP_LANES), F32), pltpu.VMEM((cdim, MLA_GROUP_LANES), F32)],
        compiler_params=_cparams("parallel", "arbitrary"),
        name="mla_prompt",
    )(q, qrt, c, ct, kr, wuk, wuv)


def _mla_sample_kernel(pt_ref, qlt_ref, qrt_ref, cn_ref, krn_ref, *refs, pages, heads):
    del pt_ref
    c_refs, krt_refs = refs[:pages], refs[pages:2 * pages]
    o_ref, m_scr, l_scr, acc_scr = refs[2 * pages:]
    g = pl.program_id(1)

    @pl.when(g == 0)
    def _():
        _flash_init(m_scr, l_scr, acc_scr)

    qlt, qrt = qlt_ref[...], qrt_ref[...]
    n_chains = m_scr.shape[0]
    per = pages // n_chains
    for ch in range(n_chains):
        cs = [c_refs[ch * per + t][...].astype(BF16) for t in range(per)]
        ss = [_dot(cs[t], qlt) + _dot_tn(krt_refs[ch * per + t][...].astype(BF16), qrt) for t in range(per)]
        _flash_update(ss, lambda idx, p, cs=cs: _dot_tn(cs[idx], p), m_scr.at[ch], l_scr.at[ch], acc_scr.at[ch])

    @pl.when(g == pl.num_programs(1) - 1)
    def _():
        cn = cn_ref[...].astype(BF16)
        s = _dot(cn, qlt) + _dot(krn_ref[...].astype(BF16), qrt)
        tok = lax.broadcasted_iota(jnp.int32, s.shape, 0)
        qi = lax.broadcasted_iota(jnp.int32, s.shape, 1) // heads
        s = jnp.where(tok <= qi, s, NEG_INF)
        _flash_update([s], lambda idx, p: _dot_tn(cn, p), m_scr.at[0], l_scr.at[0], acc_scr.at[0])
        m_all = functools.reduce(jnp.maximum, [m_scr[ch] for ch in range(n_chains)])
        ws = [jnp.exp(m_scr[ch] - m_all) for ch in range(n_chains)]
        l_all = functools.reduce(jnp.add, [ws[ch] * l_scr[ch] for ch in range(n_chains)])
        acc = functools.reduce(jnp.add, [ws[ch] * acc_scr[ch] for ch in range(n_chains)])
        o_ref[...] = (acc / l_all).T.astype(o_ref.dtype)


def mla_sample(qlt, qrt, c_new, kr_new, cache_c, cache_krt, page_table, layer, *, heads):
    n_seq, n_pages = page_table.shape
    pages = PAGES_PER_STEP
    n_new = c_new.shape[0] // n_seq
    rows = n_new * heads
    cdim, rdim = c_new.shape[1], kr_new.shape[1]
    assert n_pages % pages == 0

    def c_spec(t):
        return pl.BlockSpec((None, None, PAGE_SIZE, cdim), lambda b, g, pt: (layer, pt[b, g * pages + t], 0, 0))

    def kr_spec(t):
        return pl.BlockSpec((None, None, rdim, PAGE_SIZE), lambda b, g, pt: (layer, pt[b, g * pages + t], 0, 0))

    grid_spec = pltpu.PrefetchScalarGridSpec(
        num_scalar_prefetch=1,
        grid=(n_seq, n_pages // pages),
        in_specs=[pl.BlockSpec((None, cdim, rows), lambda b, g, pt: (b, 0, 0)),
                  pl.BlockSpec((None, rdim, rows), lambda b, g, pt: (b, 0, 0)),
                  pl.BlockSpec((n_new, cdim), lambda b, g, pt: (b, 0)),
                  pl.BlockSpec((n_new, rdim), lambda b, g, pt: (b, 0))]
        + [c_spec(t) for t in range(pages)] + [kr_spec(t) for t in range(pages)],
        out_specs=pl.BlockSpec((rows, cdim), lambda b, g, pt: (b, 0)),
        scratch_shapes=[pltpu.VMEM((MLA_DECODE_CHAINS, 1, rows), F32), pltpu.VMEM((MLA_DECODE_CHAINS, 1, rows), F32),
                        pltpu.VMEM((MLA_DECODE_CHAINS, cdim, rows), F32)],
    )
    return pl.pallas_call(
        functools.partial(_mla_sample_kernel, pages=pages, heads=heads),
        grid_spec=grid_spec,
        out_shape=jax.ShapeDtypeStruct((n_seq * rows, cdim), BF16),
        compiler_params=_cparams("parallel", "arbitrary"),
        name="mla_sample",
    )(page_table, qlt, qrt, c_new, kr_new, *([cache_c] * pages), *([cache_krt] * pages))


TM = 1024


def _hgrn_layer(x, norm_g, w_in, w_out, layer, lb, gnorm, s0, *, n_prompt, prompt_len, n_sample, sample_len):
    d = x.shape[1]
    dk = d // HG_HEADS
    p = norm_matmul(x, norm_g, w_in, layer, tm=TM, tn=512)
    kw = dict(mode="hgrn", heads=HG_HEADS, dk=dk, dv=dk, scale=dk ** -0.5)
    op, sp = lin_attn(p, (lb,), gnorm, None, n_seq=n_prompt, seq_len=prompt_len, row0=0, **kw)
    os_, ss = lin_attn(p, (lb,), gnorm, s0, n_seq=n_sample, seq_len=sample_len, row0=n_prompt * prompt_len, **kw)
    o = jnp.concatenate([op, os_], axis=0)
    return matmul_res(o, w_out, layer, x, tm=TM, tn=1024, tk=512), sp, ss


def _gla_layer(x, norm_g, w_in, w_gate_up, b_gate, gnorm, w_out, layer, s0, *, n_prompt, prompt_len, n_sample,
               sample_len):
    d = x.shape[1]
    dk, dv = d // 2 // GLA_HEADS, d // GLA_HEADS
    w_pad = jnp.pad(w_in, ((0, 0), (0, LANES - GLA_GATE_RANK))).astype(BF16)[None]
    p = norm_matmul(x, norm_g, w_pad, 0, tm=TM, tn=w_pad.shape[2] // 7)
    wg = jnp.pad(w_gate_up, ((0, LANES - GLA_GATE_RANK), (0, 0))).astype(BF16)
    kw = dict(mode="gla", heads=GLA_HEADS, dk=dk, dv=dv, scale=dk ** -0.5)
    op, sp = lin_attn(p, (wg, b_gate), gnorm, None, n_seq=n_prompt, seq_len=prompt_len, row0=0, **kw)
    os_, ss = lin_attn(p, (wg, b_gate), gnorm, s0, n_seq=n_sample, seq_len=sample_len,
                       row0=n_prompt * prompt_len, **kw)
    o = jnp.concatenate([op, os_], axis=0)
    return matmul_res(o, w_out, layer, x, tm=TM, tn=1024, tk=512), sp, ss


def _moba_layer(x, norm_g, w_in, w_out, layer, cache_k, cache_v, page_table, *, n_prompt, prompt_len, n_sample,
                sample_len):
    dh, h, hkv = MB_HEAD_DIM, MB_HEADS, MB_KV_HEADS
    tp = n_prompt * prompt_len
    p = norm_matmul(x, norm_g, w_in, layer, tm=TM, tn=512)
    k = p[:, h * dh:(h + hkv) * dh]
    v = p[:, (h + hkv) * dh:]
    ap = moba_prompt(p, n_seq=n_prompt, seq_len=prompt_len, heads=h, kv_heads=hkv, dh=dh)
    qt = (p[tp:, :h * dh] * dh ** -0.5).reshape(n_sample, sample_len, h, dh).transpose(0, 3, 2, 1)
    qt = qt.reshape(n_sample, dh, h * sample_len)
    kv_of_col = jnp.arange(h * sample_len) // (h // hkv * sample_len)
    qbd = jnp.where(kv_of_col[None, None, None, :] == jnp.arange(hkv)[None, :, None, None], qt[:, None], 0.0)
    qbd = qbd.reshape(n_sample, hkv * dh, h * sample_len).astype(BF16)
    ks = k[tp:].reshape(n_sample, sample_len * hkv, dh)
    vs = v[tp:].reshape(n_sample, sample_len * hkv, dh)
    n_layers, n_pool = cache_k.shape[:2]
    ck = cache_k.reshape(n_layers, n_pool, PAGE_SIZE * hkv, dh)
    cv = cache_v.reshape(n_layers, n_pool, PAGE_SIZE * hkv, dh)
    as_ = moba_sample(qbd, ks, vs, ck, cv, page_table, layer, kv_heads=hkv, dh=dh)
    as_ = as_.reshape(n_sample, h, sample_len, dh).transpose(0, 2, 1, 3).reshape(n_sample * sample_len, h * dh)
    a = jnp.concatenate([ap, as_.astype(BF16)], axis=0)
    x = matmul_res(a, w_out, layer, x, tm=TM, tn=1024, tk=512)
    shape_p = (n_prompt, prompt_len, hkv, dh)
    shape_s = (n_sample, sample_len, hkv, dh)
    return x, k[:tp].reshape(shape_p), v[:tp].reshape(shape_p), k[tp:].reshape(shape_s), v[tp:].reshape(shape_s)


def _rope_tables(pos):
    half = MLA_ROPE // 2
    inv = ROPE_THETA ** (-jnp.arange(half, dtype=F32) / half)
    ang = pos.astype(F32)[:, None] * inv[None, :]
    reps = LANES // half
    return jnp.tile(jnp.cos(ang), (1, reps)), jnp.tile(jnp.sin(ang), (1, reps))


def _mla_layer(x, norm_g, w_in, q_norm, w_q_up, kv_norm, w_kv_up, w_out, layer, cache_c, cache_kr, page_table, pos, *,
               n_prompt, prompt_len, n_sample, sample_len):
    t = x.shape[0]
    h, half = MLA_HEADS, MLA_ROPE // 2
    tp = n_prompt * prompt_len
    d = w_in.shape[0]
    scale = (MLA_NOPE + MLA_ROPE) ** -0.5
    qa_w, ckv_w, kr_w = w_in[:, :MLA_Q_RANK], w_in[:, MLA_Q_RANK:MLA_Q_RANK + MLA_KV_RANK], w_in[:, MLA_Q_RANK + MLA_KV_RANK:]
    qpad = 1024 - MLA_Q_RANK
    zeros = lambda n: jnp.zeros((d, n), w_in.dtype)
    w_p = jnp.concatenate([qa_w, zeros(qpad), ckv_w, kr_w[:, :half], zeros(LANES - half), kr_w[:, half:],
                           zeros(LANES - half)], axis=1).astype(BF16)[None]
    p = norm_matmul(x, norm_g, w_p, 0, tm=TM, tn=256)
    w4 = w_q_up.reshape(MLA_Q_RANK, h, MLA_NOPE + MLA_ROPE)
    w_q = jnp.concatenate([w4[:, :, :MLA_NOPE].reshape(MLA_Q_RANK, h * MLA_NOPE),
                           w4[:, :, MLA_NOPE:MLA_NOPE + half].reshape(MLA_Q_RANK, h * half),
                           w4[:, :, MLA_NOPE + half:].reshape(MLA_Q_RANK, h * half)], axis=1).astype(BF16)[None]
    q3 = norm_matmul(p, q_norm, w_q, 0, tm=TM, tn=512)
    c32 = rmsnorm_rows(p, kv_norm, col=1024 // MLA_KV_RANK, tm=TM)
    cos, sin = _rope_tables(pos)
    rc = h * MLA_NOPE // (h * half)
    qr1, qr2 = rope_pairs(q3, cos, sin, col1=rc, col2=rc + 1, width=h * half, tm=TM)
    kc = (1024 + MLA_KV_RANK) // LANES
    kr1, kr2 = rope_pairs(p, cos, sin, col1=kc, col2=kc + 1, width=LANES, tm=TM)
    kr = jnp.concatenate([kr1[:, :half], kr2[:, :half]], axis=1)
    q_rope = (jnp.concatenate([qr1.reshape(t, h, half), qr2.reshape(t, h, half)], axis=-1) * scale).astype(BF16)
    wkv = w_kv_up.reshape(MLA_KV_RANK, h, MLA_NOPE + MLA_V)
    w_uk = wkv[:, :, :MLA_NOPE] * scale
    w_uv = wkv[:, :, MLA_NOPE:].transpose(1, 0, 2).astype(BF16)
    c16, kr16 = c32.astype(BF16), kr.astype(BF16)
    tq, tk = MLA_TQ, MLA_TK
    qrt_p = q_rope[:tp].reshape(tp // tq, tq, h, MLA_ROPE).transpose(0, 3, 2, 1).reshape(tp // tq, MLA_ROPE, h * tq)
    ct = c16[:tp].reshape(n_prompt, prompt_len // tk, tk, MLA_KV_RANK).transpose(0, 1, 3, 2)
    op = mla_prompt(q3, qrt_p, c16, ct, kr16, w_uk.transpose(1, 0, 2).astype(BF16), w_uv,
                    n_seq=n_prompt, seq_len=prompt_len)
    ql_s = head_matmul(q3[tp:], w_uk.transpose(1, 2, 0).astype(BF16), tm=n_sample * sample_len)
    rows = sample_len * h
    qlt_s = ql_s.reshape(n_sample, rows, MLA_KV_RANK).transpose(0, 2, 1)
    qrt_s = q_rope[tp:].reshape(n_sample, rows, MLA_ROPE).transpose(0, 2, 1)
    os_ = mla_sample(qlt_s, qrt_s, c32[tp:], kr[tp:], cache_c, jnp.swapaxes(cache_kr, 2, 3), page_table, layer, heads=h)
    os_ = head_matmul(os_.reshape(n_sample * sample_len, h * MLA_KV_RANK), w_uv, tm=n_sample * sample_len)
    o = jnp.concatenate([op, os_], axis=0)
    x = matmul_res(o, w_out, layer, x, tm=TM, tn=1024, tk=512)
    return (x, c32[:tp].reshape(n_prompt, prompt_len, -1), kr[:tp].reshape(n_prompt, prompt_len, -1),
            c32[tp:].reshape(n_sample, sample_len, -1), kr[tp:].reshape(n_sample, sample_len, -1))


def _dense_ffn(x, norm_g, w_in, w_out, layer):
    act = swiglu_proj(x, norm_g, w_in[:, None], layer, None, tm=TM, tn=512)
    return matmul_res(act, w_out, layer, x, tm=TM, tn=1024, tk=512)


def _moe_ffn(x, norm_g, wr, w_in, w_out, layer):
    rw, h = router(x, norm_g, wr, tm=512)
    n_experts = wr.shape[1]
    plan = _moe_plan(rw, n_experts)
    xs, row_w = moe_gather(h, plan)
    act = moe_up(xs, row_w, w_in, layer, plan)
    y = moe_down(act, w_out, layer, plan)
    return moe_combine(x, y, plan, n_experts)


def kernel(x_prompt, x_sample, state_hgrn, state_gla, cache_moba_k, cache_moba_v, cache_mla_latent, cache_mla_krope,
           page_table, norm_mixer, norm_ffn, norm_final, hgrn_w_in, hgrn_lb_logits, hgrn_gnorm, hgrn_w_out, gla_w_in,
           gla_w_gate_up, gla_b_gate, gla_gnorm, gla_w_out, moba_w_in, moba_w_out, mla_w_in, mla_q_norm, mla_w_q_up,
           mla_kv_norm, mla_w_kv_up, mla_w_out, ffn_w_in, ffn_w_out, moe_router, moe_w_in, moe_w_out):
    n_prompt, prompt_len, d = x_prompt.shape
    n_sample, sample_len, _ = x_sample.shape
    past_len = page_table.shape[1] * PAGE_SIZE
    depth = norm_mixer.shape[0]
    dims = dict(n_prompt=n_prompt, prompt_len=prompt_len, n_sample=n_sample, sample_len=sample_len)
    tp = n_prompt * prompt_len
    pos = jnp.concatenate([jnp.tile(jnp.arange(prompt_len, dtype=jnp.int32), n_prompt),
                           jnp.tile(past_len + jnp.arange(sample_len, dtype=jnp.int32), n_sample)])
    lb_all = jnp.cumsum(jax.nn.softmax(hgrn_lb_logits.astype(F32), axis=0), axis=0)
    x = jnp.concatenate([x_prompt.reshape(tp, d), x_sample.reshape(-1, d)], axis=0)
    bf = lambda w: w.astype(BF16)
    hgrn_w_in, hgrn_w_out, gla_w_out, moba_w_in, moba_w_out, mla_w_out = (
        bf(hgrn_w_in), bf(hgrn_w_out), bf(gla_w_out), bf(moba_w_in), bf(moba_w_out), bf(mla_w_out))
    ffn_w_in, ffn_w_out, moe_w_in, moe_w_out = bf(ffn_w_in), bf(ffn_w_out), bf(moe_w_in), bf(moe_w_out)
    outs = {k: [] for k in ("hg_p", "hg_s", "gla_p", "gla_s", "mbk_p", "mbv_p", "mbk_s", "mbv_s",
                            "mlc_p", "mlr_p", "mlc_s", "mlr_s")}
    for i in range(depth):
        m, j = i % 4, i // 4
        if m == 0:
            x, sp, ss = _hgrn_layer(x, norm_mixer[i], hgrn_w_in, hgrn_w_out, j, lb_all[i], hgrn_gnorm[j],
                                    state_hgrn[j], **dims)
            outs["hg_p"].append(sp)
            outs["hg_s"].append(ss)
        elif m == 1:
            x, sp, ss = _gla_layer(x, norm_mixer[i], gla_w_in[j], gla_w_gate_up[j], gla_b_gate[j], gla_gnorm[j],
                                   gla_w_out, j, state_gla[j], **dims)
            outs["gla_p"].append(sp)
            outs["gla_s"].append(ss)
        elif m == 2:
            x, kp, vp, ks, vs = _moba_layer(x, norm_mixer[i], moba_w_in, moba_w_out, j, cache_moba_k, cache_moba_v,
                                            page_table, **dims)
            for key, val in zip(("mbk_p", "mbv_p", "mbk_s", "mbv_s"), (kp, vp, ks, vs)):
                outs[key].append(val)
        else:
            x, cp, rp, cs, rs = _mla_layer(x, norm_mixer[i], mla_w_in[j], mla_q_norm[j], mla_w_q_up[j], mla_kv_norm[j],
                                           mla_w_kv_up[j], mla_w_out, j, cache_mla_latent, cache_mla_krope, page_table,
                                           pos, **dims)
            for key, val in zip(("mlc_p", "mlr_p", "mlc_s", "mlr_s"), (cp, rp, cs, rs)):
                outs[key].append(val)
        if i % 2 == 0:
            x = _dense_ffn(x, norm_ffn[i], ffn_w_in, ffn_w_out, i // 2)
        else:
            x = _moe_ffn(x, norm_ffn[i], moe_router[i // 2], moe_w_in, moe_w_out, i // 2)
    y = rmsnorm_rows(x, norm_final, tm=TM)
    return (y[:tp].reshape(x_prompt.shape), y[tp:].reshape(x_sample.shape),
            *(jnp.stack(outs[k]) for k in ("hg_p", "hg_s", "gla_p", "gla_s", "mbk_p", "mbv_p", "mbk_s", "mbv_s",
                                           "mlc_p", "mlr_p", "mlc_s", "mlr_s")))
```

```python
import functools

import jax
import jax.numpy as jnp
from jax import lax
from jax.experimental import pallas as pl
from jax.experimental.pallas import tpu as pltpu

F32 = jnp.float32
BF16 = jnp.bfloat16
NEG_INF = float("-inf")

EPS = 1e-6
PAGE_SIZE = 128
HG_HEADS = 16
GLA_HEADS = 4
GLA_GATE_RANK = 16
GLA_GATE_NORM = 16.0
MB_HEADS = 16
MB_KV_HEADS = 4
MB_HEAD_DIM = 128
MB_BLOCK = 256
MB_TOPK = 3
MLA_HEADS = 16
MLA_Q_RANK = 768
MLA_KV_RANK = 512
MLA_NOPE = 128
MLA_ROPE = 64
MLA_V = 128
ROPE_THETA = 10000.0

LANES = 128
SUBLANES = 8
VMEM_LIMIT = 48 * 2 ** 20
LA_CHUNK = 64
PAGES_PER_STEP = 16
STATE_BLOCK_BYTES = 4 * 2 ** 20
MLA_TQ = 128
MLA_TK = 512
MLA_GROUP_LANES = 1024
MLA_DECODE_CHAINS = 2
MOE_TILE = 256
MOE_CHUNK = 1024
MOE_UP_SPLITS = 2
MOE_COMBINE_GROUP = 4


def _cparams(*sem):
    return pltpu.CompilerParams(dimension_semantics=sem, vmem_limit_bytes=VMEM_LIMIT)


def _dot(a, b):
    return jnp.dot(a, b, preferred_element_type=F32)


def _dot_nt(a, b):
    return lax.dot_general(a, b, (((1,), (1,)), ((), ())), preferred_element_type=F32)


def _dot_tn(a, b):
    return lax.dot_general(a, b, (((0,), (0,)), ((), ())), preferred_element_type=F32)


def _norm_rows(x, g):
    return x * lax.rsqrt(jnp.mean(x * x, axis=-1, keepdims=True) + EPS) * g


def _lane_col(x, j):
    lane = lax.broadcasted_iota(jnp.int32, x.shape, 1)
    return jnp.sum(jnp.where(lane == j, x, 0.0), axis=-1, keepdims=True)


def _round_up(n, m):
    return -(-n // m) * m


def _norm_matmul_kernel(x_ref, g_ref, w_ref, o_ref, h_ref):
    @pl.when(pl.program_id(1) == 0)
    def _():
        h_ref[...] = _norm_rows(x_ref[...], g_ref[...]).astype(BF16)

    o_ref[...] = _dot(h_ref[...], w_ref[...]).astype(o_ref.dtype)


def norm_matmul(x, g, w, layer, *, tm, tn, out_dtype=F32):
    m = x.shape[0]
    _, k, n = w.shape
    return pl.pallas_call(
        _norm_matmul_kernel,
        grid=(m // tm, n // tn),
        in_specs=[pl.BlockSpec((tm, k), lambda i, j: (i, 0)),
                  pl.BlockSpec((1, k), lambda i, j: (0, 0)),
                  pl.BlockSpec((None, k, tn), lambda i, j: (layer, 0, j))],
        out_specs=pl.BlockSpec((tm, tn), lambda i, j: (i, j)),
        out_shape=jax.ShapeDtypeStruct((m, n), out_dtype),
        scratch_shapes=[pltpu.VMEM((tm, k), BF16)],
        compiler_params=_cparams("parallel", "arbitrary"),
        name="norm_matmul",
    )(x, g.reshape(1, k).astype(F32), w)


def _swiglu_kernel(*refs, has_norm, has_scale, nf):
    refs = list(refs)
    x_ref = refs.pop(0)
    g_ref = refs.pop(0) if has_norm else None
    wg_ref, wu_ref = refs.pop(0), refs.pop(0)
    rw_ref = refs.pop(0) if has_scale else None
    o_ref = refs.pop(0)
    if has_norm:
        h_ref = refs.pop(0)

        @pl.when(pl.program_id(1) == 0)
        def _():
            h_ref[...] = _norm_rows(x_ref[...], g_ref[...]).astype(BF16)

        h = h_ref[...]
    else:
        h = x_ref[...]
    a = _dot(h, wg_ref[...])
    u = _dot(h, wu_ref[...])
    act = a * jax.nn.sigmoid(a) * u
    if has_scale:
        act = act * _lane_col(rw_ref[...], pl.program_id(1) // nf)
    o_ref[...] = act.astype(o_ref.dtype)


def swiglu_proj(x, g, w, layer, rw, *, tm, tn):
    m, k = x.shape
    _, e, _, f2 = w.shape
    f = f2 // 2
    nf = f // tn
    has_norm, has_scale = g is not None, rw is not None
    in_specs = [pl.BlockSpec((tm, k), lambda i, j: (i, 0))]
    args = [x]
    if has_norm:
        in_specs.append(pl.BlockSpec((1, k), lambda i, j: (0, 0)))
        args.append(g.reshape(1, k).astype(F32))
    in_specs += [pl.BlockSpec((None, None, k, tn), lambda i, j: (layer, j // nf, 0, j % nf)),
                 pl.BlockSpec((None, None, k, tn), lambda i, j: (layer, j // nf, 0, nf + j % nf))]
    args += [w, w]
    if has_scale:
        in_specs.append(pl.BlockSpec((tm, LANES), lambda i, j: (i, 0)))
        args.append(rw)
    return pl.pallas_call(
        functools.partial(_swiglu_kernel, has_norm=has_norm, has_scale=has_scale, nf=nf),
        grid=(m // tm, e * nf),
        in_specs=in_specs,
        out_specs=pl.BlockSpec((tm, tn), lambda i, j: (i, j)),
        out_shape=jax.ShapeDtypeStruct((m, e * f), BF16),
        scratch_shapes=[pltpu.VMEM((tm, k), BF16)] if has_norm else [],
        compiler_params=_cparams("parallel", "arbitrary"),
        name="swiglu_proj",
    )(*args)


def _matmul_res_kernel(a_ref, w_ref, r_ref, o_ref):
    @pl.when(pl.program_id(2) == 0)
    def _():
        o_ref[...] = r_ref[...]

    o_ref[...] += _dot(a_ref[...], w_ref[...])


def matmul_res(a, w, layer, res, *, tm, tn, tk):
    m, k = a.shape
    n = w.shape[2]
    return pl.pallas_call(
        _matmul_res_kernel,
        grid=(m // tm, n // tn, k // tk),
        in_specs=[pl.BlockSpec((tm, tk), lambda i, j, kk: (i, kk)),
                  pl.BlockSpec((None, tk, tn), lambda i, j, kk: (layer, kk, j)),
                  pl.BlockSpec((tm, tn), lambda i, j, kk: (i, j))],
        out_specs=pl.BlockSpec((tm, tn), lambda i, j, kk: (i, j)),
        out_shape=jax.ShapeDtypeStruct((m, n), F32),
        compiler_params=_cparams("parallel", "parallel", "arbitrary"),
        name="matmul_res",
    )(a, w, res)


def _rmsnorm_kernel(x_ref, g_ref, o_ref):
    o_ref[...] = _norm_rows(x_ref[...], g_ref[...]).astype(o_ref.dtype)


def rmsnorm_rows(x, g, *, col=0, tm, out_dtype=F32):
    m = x.shape[0]
    k = g.shape[-1]
    return pl.pallas_call(
        _rmsnorm_kernel,
        grid=(m // tm,),
        in_specs=[pl.BlockSpec((tm, k), lambda i: (i, col)),
                  pl.BlockSpec((1, k), lambda i: (0, 0))],
        out_specs=pl.BlockSpec((tm, k), lambda i: (i, 0)),
        out_shape=jax.ShapeDtypeStruct((m, k), out_dtype),
        compiler_params=_cparams("parallel"),
        name="rmsnorm_rows",
    )(x, g.reshape(1, k).astype(F32))


def _router_kernel(x_ref, g_ref, wr_ref, rw_ref, h_ref, *, n_experts):
    h = _norm_rows(x_ref[...], g_ref[...])
    h_ref[...] = h.astype(BF16)
    logits = jnp.dot(h, wr_ref[...], preferred_element_type=F32, precision=lax.Precision.HIGHEST)
    lane = lax.broadcasted_iota(jnp.int32, logits.shape, 1)
    l1 = jnp.where(lane < n_experts, logits, NEG_INF)
    m1 = jnp.max(l1, axis=-1, keepdims=True)
    i1 = jnp.min(jnp.where(l1 == m1, lane, LANES), axis=-1, keepdims=True)
    l2 = jnp.where(lane == i1, NEG_INF, l1)
    m2 = jnp.max(l2, axis=-1, keepdims=True)
    i2 = jnp.min(jnp.where(l2 == m2, lane, LANES), axis=-1, keepdims=True)
    e2 = jnp.exp(m2 - m1)
    den = 1.0 + e2
    rw_ref[...] = jnp.where(lane == i1, 1.0 / den, 0.0) + jnp.where(lane == i2, e2 / den, 0.0)


def router(x, g, wr, *, tm):
    m, k = x.shape
    n_experts = wr.shape[1]
    wr_pad = jnp.pad(wr.astype(F32), ((0, 0), (0, LANES - n_experts)))
    return pl.pallas_call(
        functools.partial(_router_kernel, n_experts=n_experts),
        grid=(m // tm,),
        in_specs=[pl.BlockSpec((tm, k), lambda i: (i, 0)),
                  pl.BlockSpec((1, k), lambda i: (0, 0)),
                  pl.BlockSpec((k, LANES), lambda i: (0, 0))],
        out_specs=[pl.BlockSpec((tm, LANES), lambda i: (i, 0)),
                   pl.BlockSpec((tm, k), lambda i: (i, 0))],
        out_shape=[jax.ShapeDtypeStruct((m, LANES), F32), jax.ShapeDtypeStruct((m, k), BF16)],
        compiler_params=_cparams("parallel"),
        name="router",
    )(x, g.reshape(1, k).astype(F32), wr_pad)


def _moe_plan(rw, n_experts):
    t = rw.shape[0]
    tile, chunk = MOE_TILE, MOE_CHUNK
    n_tiles = 2 * t // tile + n_experts
    n_rows = n_tiles * tile
    w = rw[:, :n_experts]
    mask = w > 0
    cnt = jnp.sum(mask, axis=0, dtype=jnp.int32)
    pos = jnp.cumsum(mask, axis=0, dtype=jnp.int32) - 1
    tiles_e = (cnt + tile - 1) // tile
    ends = jnp.cumsum(tiles_e)
    dest = jnp.where(mask, ((ends - tiles_e) * tile)[None, :] + pos, -1)
    tile_ids = jnp.arange(n_tiles, dtype=jnp.int32)
    tile_expert = jnp.minimum(jnp.sum(ends[None, :] <= tile_ids[:, None], axis=1), n_experts - 1).astype(jnp.int32)
    dc = dest.reshape(t // chunk, chunk, n_experts)
    c_hi_tile = jnp.max(dc, axis=1) // tile
    c_lo_tile = jnp.min(jnp.where(dc >= 0, dc, n_rows), axis=1) // tile
    lo_sel = jnp.take(c_lo_tile.T, tile_expert, axis=0)
    hi_sel = jnp.take(c_hi_tile.T, tile_expert, axis=0)
    covers = (hi_sel >= 0) & (lo_sel <= tile_ids[:, None]) & (tile_ids[:, None] <= hi_sel)
    cidx = jnp.arange(t // chunk, dtype=jnp.int32)[None, :]
    first = jnp.min(jnp.where(covers, cidx, t // chunk), axis=1)
    last = jnp.max(jnp.where(covers, cidx, -1), axis=1)
    c_lo = jnp.where(last >= 0, first, 0).astype(jnp.int32)
    n_c = jnp.where(last >= 0, last - first + 1, 0).astype(jnp.int32)
    d3 = dest.reshape(t // tile, tile, n_experts)
    lo = jnp.min(jnp.where(d3 >= 0, d3, n_rows), axis=1)
    hi = jnp.max(d3, axis=1)
    b0 = jnp.where(hi >= 0, lo // tile, 0).astype(jnp.int32).reshape(-1)
    nb = jnp.where(hi >= 0, hi // tile - lo // tile + 1, 0).astype(jnp.int32).reshape(-1)
    dest_f = dest.astype(F32)
    pad = _round_up(n_experts, SUBLANES) - n_experts
    return dict(n_tiles=n_tiles, tile_expert=tile_expert, n_used=ends[-1:].astype(jnp.int32), c_lo=c_lo, n_c=n_c,
                b0=b0, nb=nb,
                dest=jnp.pad(dest_f, ((0, 0), (0, LANES - n_experts)), constant_values=-1.0),
                dest_t=jnp.pad(dest_f.T, ((0, pad), (0, 0)), constant_values=-1.0),
                w_t=jnp.pad(w.T, ((0, pad), (0, 0))))


def _moe_gather_kernel(te_ref, clo_ref, nc_ref, dest_ref, w_ref, h_ref, o_ref, rw_ref, acc_scr, rw_scr):
    r, c = pl.program_id(0), pl.program_id(1)

    @pl.when(c == 0)
    def _():
        acc_scr[...] = jnp.zeros_like(acc_scr)
        rw_scr[...] = jnp.zeros_like(rw_scr)

    @pl.when(c < nc_ref[r])
    def _():
        tile = o_ref.shape[0]
        e = te_ref[r]
        rel = dest_ref[pl.ds(e, 1), :] - (r * tile).astype(F32)
        row = lax.broadcasted_iota(jnp.int32, (tile, rel.shape[1]), 0).astype(F32)
        hit = rel == row
        acc_scr[...] += _dot(jnp.where(hit, 1.0, 0.0).astype(BF16), h_ref[...])
        rw_scr[...] += jnp.sum(jnp.where(hit, w_ref[pl.ds(e, 1), :], 0.0), axis=-1, keepdims=True)

    @pl.when(c == pl.num_programs(1) - 1)
    def _():
        o_ref[...] = acc_scr[...].astype(o_ref.dtype)
        rw_ref[...] = rw_scr[...]


def moe_gather(h, plan):
    t, k = h.shape
    tile, chunk = MOE_TILE, MOE_CHUNK
    n_tiles = plan["n_tiles"]
    ne = plan["dest_t"].shape[0]

    def chunk_of(r, c, te, clo, nc):
        return clo[r] + jnp.minimum(c, jnp.maximum(nc[r], 1) - 1)

    grid_spec = pltpu.PrefetchScalarGridSpec(
        num_scalar_prefetch=3,
        grid=(n_tiles, t // chunk),
        in_specs=[pl.BlockSpec((ne, chunk), lambda *a: (0, chunk_of(*a))),
                  pl.BlockSpec((ne, chunk), lambda *a: (0, chunk_of(*a))),
                  pl.BlockSpec((chunk, k), lambda *a: (chunk_of(*a), 0))],
        out_specs=[pl.BlockSpec((tile, k), lambda r, c, te, clo, nc: (r, 0)),
                   pl.BlockSpec((tile, 1), lambda r, c, te, clo, nc: (r, 0))],
        scratch_shapes=[pltpu.VMEM((tile, k), F32), pltpu.VMEM((tile, 1), F32)],
    )
    return pl.pallas_call(
        _moe_gather_kernel,
        grid_spec=grid_spec,
        out_shape=[jax.ShapeDtypeStruct((n_tiles * tile, k), BF16), jax.ShapeDtypeStruct((n_tiles * tile, 1), F32)],
        compiler_params=_cparams("parallel", "arbitrary"),
        name="moe_gather",
    )(plan["tile_expert"], plan["c_lo"], plan["n_c"], plan["dest_t"], plan["w_t"], h)


def _moe_up_kernel(te_ref, nu_ref, x_ref, wg_ref, wu_ref, rw_ref, o_ref):
    del te_ref
    used = pl.program_id(1) < nu_ref[0]

    @pl.when(used)
    def _():
        h = x_ref[...]
        a = _dot(h, wg_ref[...])
        u = _dot(h, wu_ref[...])
        o_ref[...] = (a * jax.nn.sigmoid(a) * u * rw_ref[...]).astype(o_ref.dtype)

    @pl.when(jnp.logical_not(used))
    def _():
        o_ref[...] = jnp.zeros_like(o_ref)


def moe_up(xs, row_w, w, layer, plan):
    n_rows, k = xs.shape
    f = w.shape[3] // 2
    tile = MOE_TILE
    nf = MOE_UP_SPLITS
    tn = f // nf
    grid_spec = pltpu.PrefetchScalarGridSpec(
        num_scalar_prefetch=2,
        grid=(nf, n_rows // tile),
        in_specs=[pl.BlockSpec((tile, k), lambda j, r, te, nu: (r, 0)),
                  pl.BlockSpec((None, None, k, tn), lambda j, r, te, nu: (layer, te[r], 0, j)),
                  pl.BlockSpec((None, None, k, tn), lambda j, r, te, nu: (layer, te[r], 0, nf + j)),
                  pl.BlockSpec((tile, 1), lambda j, r, te, nu: (r, 0))],
        out_specs=pl.BlockSpec((tile, tn), lambda j, r, te, nu: (r, j)),
    )
    return pl.pallas_call(
        _moe_up_kernel,
        grid_spec=grid_spec,
        out_shape=jax.ShapeDtypeStruct((n_rows, f), BF16),
        compiler_params=_cparams("arbitrary", "arbitrary"),
        name="moe_up",
    )(plan["tile_expert"], plan["n_used"], xs, w, w, row_w)


def _moe_down_kernel(te_ref, nu_ref, a_ref, w_ref, o_ref):
    del te_ref
    used = pl.program_id(0) < nu_ref[0]

    @pl.when(used)
    def _():
        o_ref[...] = _dot(a_ref[...], w_ref[...]).astype(o_ref.dtype)

    @pl.when(jnp.logical_not(used))
    def _():
        o_ref[...] = jnp.zeros_like(o_ref)


def moe_down(act, w, layer, plan):
    n_rows, f = act.shape
    d = w.shape[3]
    tile = MOE_TILE
    grid_spec = pltpu.PrefetchScalarGridSpec(
        num_scalar_prefetch=2,
        grid=(n_rows // tile,),
        in_specs=[pl.BlockSpec((tile, f), lambda r, te, nu: (r, 0)),
                  pl.BlockSpec((None, None, f, d), lambda r, te, nu: (layer, te[r], 0, 0))],
        out_specs=pl.BlockSpec((tile, d), lambda r, te, nu: (r, 0)),
    )
    return pl.pallas_call(
        _moe_down_kernel,
        grid_spec=grid_spec,
        out_shape=jax.ShapeDtypeStruct((n_rows, d), BF16),
        compiler_params=_cparams("arbitrary"),
        name="moe_down",
    )(plan["tile_expert"], plan["n_used"], act, w)


def _moe_combine_kernel(b0_ref, nb_ref, dest_ref, x_ref, *refs, n_experts, group):
    y_refs, o_ref = refs[:-1], refs[-1]
    tt, eg = pl.program_id(0), pl.program_id(1)
    tile = o_ref.shape[0]
    lane = lax.broadcasted_iota(jnp.int32, (tile, tile), 1).astype(F32)
    dest = dest_ref[...]
    parts = []
    for ee in range(group):
        e = eg * group + ee
        col = _lane_col(dest, e)
        for kk in range(2):
            rel = col - ((b0_ref[tt * n_experts + e] + kk) * tile).astype(F32)
            parts.append(_dot(jnp.where(rel == lane, 1.0, 0.0).astype(BF16), y_refs[ee * 2 + kk][...]))
    total = functools.reduce(jnp.add, parts)

    @pl.when(eg == 0)
    def _():
        o_ref[...] = x_ref[...] + total

    @pl.when(eg != 0)
    def _():
        o_ref[...] += total


def moe_combine(x, y, plan, n_experts):
    t, d = x.shape
    tile = MOE_TILE
    group = MOE_COMBINE_GROUP
    n_blocks = y.shape[0] // tile

    def y_spec(ee, kk):
        def y_map(tt, eg, b0, nb):
            idx = tt * n_experts + eg * group + ee
            return (jnp.minimum(b0[idx] + jnp.minimum(kk, jnp.maximum(nb[idx], 1) - 1), n_blocks - 1), 0)
        return pl.BlockSpec((tile, d), y_map)

    grid_spec = pltpu.PrefetchScalarGridSpec(
        num_scalar_prefetch=2,
        grid=(t // tile, n_experts // group),
        in_specs=[pl.BlockSpec((tile, LANES), lambda tt, eg, b0, nb: (tt, 0)),
                  pl.BlockSpec((tile, d), lambda tt, eg, b0, nb: (tt, 0))]
        + [y_spec(ee, kk) for ee in range(group) for kk in range(2)],
        out_specs=pl.BlockSpec((tile, d), lambda tt, eg, b0, nb: (tt, 0)),
    )
    return pl.pallas_call(
        functools.partial(_moe_combine_kernel, n_experts=n_experts, group=group),
        grid_spec=grid_spec,
        out_shape=jax.ShapeDtypeStruct((t, d), F32),
        compiler_params=_cparams("parallel", "arbitrary"),
        name="moe_combine",
    )(plan["b0"], plan["nb"], plan["dest"], x, *([y] * (2 * group)))


def _head_matmul_kernel(a_ref, w_ref, o_ref):
    o_ref[...] = _dot(a_ref[...].astype(BF16), w_ref[...]).astype(o_ref.dtype)


def head_matmul(a, w, *, tm, out_dtype=BF16):
    m = a.shape[0]
    h, k, n = w.shape
    return pl.pallas_call(
        _head_matmul_kernel,
        grid=(m // tm, h),
        in_specs=[pl.BlockSpec((tm, k), lambda i, j: (i, j)),
                  pl.BlockSpec((None, k, n), lambda i, j: (j, 0, 0))],
        out_specs=pl.BlockSpec((tm, n), lambda i, j: (i, j)),
        out_shape=jax.ShapeDtypeStruct((m, h * n), out_dtype),
        compiler_params=_cparams("parallel", "arbitrary"),
        name="head_matmul",
    )(a, w)


def _rope_kernel(x1_ref, x2_ref, cos_ref, sin_ref, r1_ref, r2_ref, *, reps):
    cos, sin = cos_ref[...], sin_ref[...]
    if reps > 1:
        cos = jnp.concatenate([cos] * reps, axis=-1)
        sin = jnp.concatenate([sin] * reps, axis=-1)
    x1, x2 = x1_ref[...], x2_ref[...]
    r1_ref[...] = (x1 * cos - x2 * sin).astype(r1_ref.dtype)
    r2_ref[...] = (x1 * sin + x2 * cos).astype(r2_ref.dtype)


def rope_pairs(x, cos, sin, *, col1, col2, width, tm):
    m = x.shape[0]
    return pl.pallas_call(
        functools.partial(_rope_kernel, reps=width // LANES),
        grid=(m // tm,),
        in_specs=[pl.BlockSpec((tm, width), lambda i: (i, col1)),
                  pl.BlockSpec((tm, width), lambda i: (i, col2)),
                  pl.BlockSpec((tm, LANES), lambda i: (i, 0)),
                  pl.BlockSpec((tm, LANES), lambda i: (i, 0))],
        out_specs=[pl.BlockSpec((tm, width), lambda i: (i, 0))] * 2,
        out_shape=[jax.ShapeDtypeStruct((m, width), F32)] * 2,
        compiler_params=_cparams("parallel"),
        name="rope_pairs",
    )(x, x, cos, sin)


def _cumsum_rows(x, c):
    row = lax.broadcasted_iota(jnp.int32, x.shape, 0)
    sh = 1
    while sh < c:
        x = x + jnp.where(row >= sh, pltpu.roll(x, sh, 0), 0.0)
        sh *= 2
    return x


def _level_masks(c):
    t = lax.broadcasted_iota(jnp.int32, (c, c), 0)
    s = lax.broadcasted_iota(jnp.int32, (c, c), 1)
    masks = []
    blk = 2 * SUBLANES
    while blk <= c:
        half = blk // 2
        masks.append((t // blk == s // blk) & (t % blk >= half) & (s % blk < half))
        blk *= 2
    return masks


def _gla_chunk(q, k, v, lf, s_prev, c, masks):
    dk = q.shape[1]
    b = _cumsum_rows(lf, c)
    rid = lax.broadcasted_iota(jnp.int32, (c, 1), 0) % SUBLANES
    o = jnp.sum(q * k, axis=-1, keepdims=True) * v
    for d in range(1, SUBLANES):
        kd, bd, vd = pltpu.roll(k, d, 0), pltpu.roll(b, d, 0), pltpu.roll(v, d, 0)
        w = jnp.sum(q * kd * jnp.exp(jnp.minimum(b - bd, 0.0)), axis=-1, keepdims=True)
        o = o + jnp.where(rid >= d, w, 0.0) * vd
    if c > SUBLANES:
        a = jnp.zeros((c, c), F32)
        blk = 2 * SUBLANES
        for pair in masks:
            half = blk // 2
            ref = jnp.concatenate(
                [jnp.broadcast_to(b[i * blk + half - 1:i * blk + half, :], (blk, dk)) for i in range(c // blk)], axis=0)
            qt = (q * jnp.exp(jnp.minimum(b - ref, 0.0))).astype(BF16)
            kt = (k * jnp.exp(jnp.minimum(ref - b, 0.0))).astype(BF16)
            a = a + jnp.where(pair, _dot_nt(qt, kt), 0.0)
            blk *= 2
        o = o + _dot(a.astype(BF16), v.astype(BF16))
    o = o + _dot((q * jnp.exp(b)).astype(BF16), s_prev.astype(BF16))
    b_end = b[c - 1:c, :]
    kt = (k * jnp.exp(b_end - b)).astype(BF16)
    decay = jnp.exp(jnp.sum(lf.T, axis=-1, keepdims=True))
    s_new = decay * s_prev + _dot_tn(kt, v.astype(BF16))
    return o, s_new


def _lin_attn_kernel(*refs, mode, sample, c, n_inner, scale):
    refs = list(refs)
    if mode == "hgrn":
        q_ref, f_ref, v_ref, g_ref, lb_ref, gn_ref = refs[:6]
        refs = refs[6:]
    else:
        q_ref, k_ref, v_ref, g_ref, gl_ref, wg_ref, bg_ref, gn_ref = refs[:8]
        refs = refs[8:]
    if sample:
        s0_ref, o_ref, so_ref = refs
    else:
        o_ref, so_ref, s_scr = refs

    def prep(rows):
        if mode == "hgrn":
            qr, fr, lb = q_ref[rows, :], f_ref[rows, :], lb_ref[...]
            q = qr * jax.nn.sigmoid(qr) * scale
            lf = jnp.log(lb + (1.0 - lb) * jax.nn.sigmoid(fr))
            k = (1.0 - lb) * jax.nn.sigmoid(-fr)
        else:
            q = q_ref[rows, :] * scale
            k = k_ref[rows, :]
            z = _dot(gl_ref[rows, :].astype(BF16), wg_ref[...]) + bg_ref[...]
            lf = -(jnp.maximum(-z, 0.0) + jnp.log1p(jnp.exp(-jnp.abs(z)))) * (1.0 / GLA_GATE_NORM)
        return q, k, v_ref[rows, :], lf

    def finish(rows, o):
        g = g_ref[rows, :]
        o_ref[rows, :] = (_norm_rows(o, gn_ref[...]) * (g * jax.nn.sigmoid(g))).astype(o_ref.dtype)

    masks = _level_masks(c)

    if sample:
        def body(si, carry):
            rows = pl.ds(pl.multiple_of(si * c, c), c)
            q, k, v, lf = prep(rows)
            o, s_new = _gla_chunk(q, k, v, lf, s0_ref[si], c, masks)
            so_ref[si] = s_new
            finish(rows, o)
            return carry
    else:
        @pl.when(pl.program_id(2) == 0)
        def _():
            s_scr[...] = jnp.zeros_like(s_scr)

        def body(ci, carry):
            rows = pl.ds(pl.multiple_of(ci * c, c), c)
            q, k, v, lf = prep(rows)
            o, s_new = _gla_chunk(q, k, v, lf, s_scr[...], c, masks)
            s_scr[...] = s_new
            finish(rows, o)
            return carry

    lax.fori_loop(0, n_inner, body, 0, unroll=2)

    if not sample:
        @pl.when(pl.program_id(2) == pl.num_programs(2) - 1)
        def _():
            so_ref[...] = s_scr[...]


def lin_attn(p, extra, gnorm, s0, *, mode, heads, dk, dv, n_seq, seq_len, row0, scale):
    sample = s0 is not None
    if sample:
        c = seq_len
        nb = max(1, min(n_seq, STATE_BLOCK_BYTES // (dk * dv * 4)))
        tb, n_inner = nb * c, nb
        grid = (n_seq // nb, heads, 1)
        rb = row0 // tb
        row = lambda b, h, i: rb + b
    else:
        c, tb = LA_CHUNK, 4 * LA_CHUNK
        n_inner = tb // c
        nt = seq_len // tb
        grid = (n_seq, heads, nt)
        rb = row0 // tb
        row = lambda b, h, i: rb + b * nt + i
    if mode == "hgrn":
        lb, = extra
        nh = heads
        in_specs = [pl.BlockSpec((tb, dk), lambda b, h, i: (row(b, h, i), h)),
                    pl.BlockSpec((tb, dk), lambda b, h, i: (row(b, h, i), nh + h)),
                    pl.BlockSpec((tb, dv), lambda b, h, i: (row(b, h, i), 2 * nh + h)),
                    pl.BlockSpec((tb, dv), lambda b, h, i: (row(b, h, i), 3 * nh + h)),
                    pl.BlockSpec((1, dk), lambda b, h, i: (0, h)),
                    pl.BlockSpec((1, dv), lambda b, h, i: (0, 0))]
        args = [p, p, p, p, lb.reshape(1, -1).astype(F32), gnorm.reshape(1, dv).astype(F32)]
    else:
        wg, bg = extra
        nh = heads
        voff = 2 * heads * dk // dv
        gloff = (2 * heads * dk + 2 * heads * dv) // LANES
        in_specs = [pl.BlockSpec((tb, dk), lambda b, h, i: (row(b, h, i), h)),
                    pl.BlockSpec((tb, dk), lambda b, h, i: (row(b, h, i), nh + h)),
                    pl.BlockSpec((tb, dv), lambda b, h, i: (row(b, h, i), voff + h)),
                    pl.BlockSpec((tb, dv), lambda b, h, i: (row(b, h, i), voff + nh + h)),
                    pl.BlockSpec((tb, LANES), lambda b, h, i: (row(b, h, i), gloff)),
                    pl.BlockSpec((LANES, dk), lambda b, h, i: (0, h)),
                    pl.BlockSpec((1, dk), lambda b, h, i: (0, h)),
                    pl.BlockSpec((1, dv), lambda b, h, i: (0, 0))]
        args = [p, p, p, p, p, wg, bg.reshape(1, -1).astype(F32), gnorm.reshape(1, dv).astype(F32)]
    out_specs = [pl.BlockSpec((tb, dv), lambda b, h, i: (row(b, h, i) - rb, h))]
    out_shape = [jax.ShapeDtypeStruct((n_seq * seq_len, heads * dv), BF16)]
    scratch = []
    if sample:
        in_specs.append(pl.BlockSpec((nb, None, dk, dv), lambda b, h, i: (b, h, 0, 0)))
        args.append(s0)
        out_specs.append(pl.BlockSpec((nb, None, dk, dv), lambda b, h, i: (b, h, 0, 0)))
    else:
        out_specs.append(pl.BlockSpec((None, None, dk, dv), lambda b, h, i: (b, h, 0, 0)))
        scratch.append(pltpu.VMEM((dk, dv), F32))
    out_shape.append(jax.ShapeDtypeStruct((n_seq, heads, dk, dv), F32))
    return pl.pallas_call(
        functools.partial(_lin_attn_kernel, mode=mode, sample=sample, c=c, n_inner=n_inner, scale=scale),
        grid=grid,
        in_specs=in_specs,
        out_specs=out_specs,
        out_shape=out_shape,
        scratch_shapes=scratch,
        compiler_params=_cparams("parallel", "parallel", "arbitrary"),
        name=f"lin_attn_{mode}_{'sample' if sample else 'prompt'}",
    )(*args)


def _moba_weights(gate, m_all, n_prev, own, n_static):
    ridx = lax.broadcasted_iota(jnp.int32, gate.shape, 0)
    cnt = jnp.zeros(gate.shape, F32)
    for jp in range(n_static):
        row = gate[jp:jp + 1, :]
        beats = (row > gate) | ((row == gate) & (jp < ridx))
        cnt = cnt + jnp.where(beats & (jp < n_prev), 1.0, 0.0)
    sel = ((cnt < MB_TOPK) & (ridx < n_prev)) | (ridx == own)
    mv = jnp.where(sel, m_all, NEG_INF)
    return jnp.where(sel, jnp.exp(mv - jnp.max(mv, axis=0, keepdims=True)), 0.0)


def _init_partials(m_scr, l_scr, g_scr):
    m_scr[...] = jnp.full(m_scr.shape, NEG_INF, F32)
    g_scr[...] = jnp.full(g_scr.shape, NEG_INF, F32)
    l_scr[...] = jnp.zeros(l_scr.shape, F32)


def _moba_prompt_kernel(q_ref, k_ref, v_ref, o_ref, qt_scr, m_scr, l_scr, g_scr, w_scr, o_scr, *, scale, group, n_blk):
    i = pl.program_id(2)
    dh = k_ref.shape[1]
    for hh in range(group):
        qt_scr[hh] = (q_ref[:, hh * dh:(hh + 1) * dh] * scale).T.astype(BF16)
    _init_partials(m_scr, l_scr, g_scr)
    kidx = lax.broadcasted_iota(jnp.int32, (MB_BLOCK, MB_BLOCK), 0)
    qidx = lax.broadcasted_iota(jnp.int32, (MB_BLOCK, MB_BLOCK), 1)
    causal = kidx <= qidx

    def blk(j, carry):
        rows = pl.ds(pl.multiple_of(j * MB_BLOCK, MB_BLOCK), MB_BLOCK)
        kj, vj = k_ref[rows, :].astype(BF16), v_ref[rows, :].astype(BF16)
        keep = causal | (j < i)
        for hh in range(group):
            s = _dot(kj, qt_scr[hh])
            g_scr[hh, pl.ds(j, 1), :] = jnp.sum(s, axis=0, keepdims=True)
            s = jnp.where(keep, s, NEG_INF)
            m = jnp.max(s, axis=0, keepdims=True)
            p = jnp.exp(s - m)
            m_scr[hh, pl.ds(j, 1), :] = m
            l_scr[hh, pl.ds(j, 1), :] = jnp.sum(p, axis=0, keepdims=True)
            o_scr[hh, j] = _dot_tn(vj, p.astype(BF16))
        return carry

    lax.fori_loop(0, i + 1, blk, 0)
    for hh in range(group):
        w = _moba_weights(g_scr[hh], m_scr[hh], i, i, n_blk)
        den = jnp.sum(w * l_scr[hh], axis=0, keepdims=True)
        w_scr[...] = w

        def comb(j, acc):
            return acc + w_scr[pl.ds(j, 1), :] * o_scr[hh, j]

        acc = lax.fori_loop(0, i + 1, comb, jnp.zeros((dh, MB_BLOCK), F32))
        o_ref[:, hh * dh:(hh + 1) * dh] = (acc / den).T.astype(o_ref.dtype)


def moba_prompt(p, *, n_seq, seq_len, heads, kv_heads, dh):
    group = heads // kv_heads
    nq = seq_len // MB_BLOCK
    nbp = _round_up(nq, SUBLANES)
    koff = heads
    voff = heads + kv_heads
    return pl.pallas_call(
        functools.partial(_moba_prompt_kernel, scale=dh ** -0.5, group=group, n_blk=nq),
        grid=(n_seq, kv_heads, nq),
        in_specs=[pl.BlockSpec((MB_BLOCK, group * dh), lambda b, g, i: (b * nq + i, g)),
                  pl.BlockSpec((seq_len, dh), lambda b, g, i: (b, koff + g)),
                  pl.BlockSpec((seq_len, dh), lambda b, g, i: (b, voff + g))],
        out_specs=pl.BlockSpec((MB_BLOCK, group * dh), lambda b, g, i: (b * nq + i, g)),
        out_shape=jax.ShapeDtypeStruct((n_seq * seq_len, heads * dh), BF16),
        scratch_shapes=[pltpu.VMEM((group, dh, MB_BLOCK), BF16)]
        + [pltpu.VMEM((group, nbp, MB_BLOCK), F32)] * 3
        + [pltpu.VMEM((nbp, MB_BLOCK), F32), pltpu.VMEM((group, nq, dh, MB_BLOCK), F32)],
        compiler_params=_cparams("parallel", "parallel", "arbitrary"),
        name="moba_prompt",
    )(p, p, p)


def _moba_sample_kernel(pt_ref, qbd_ref, kn_ref, vn_ref, *refs, pages, kv_heads, n_new):
    del pt_ref
    k_refs, v_refs = refs[:pages], refs[pages:2 * pages]
    o_ref, m_scr, l_scr, g_scr, o_scr = refs[2 * pages:]
    g = pl.program_id(1)
    rows, dh = o_ref.shape
    rpk = rows // kv_heads
    ppb = MB_BLOCK // PAGE_SIZE
    bps = pages // ppb

    def scores(k_ref, n):
        parts = [_dot(k_ref[pl.ds(gk, n, stride=kv_heads), :].astype(BF16), qbd_ref[gk * dh:(gk + 1) * dh, :])
                 for gk in range(kv_heads)]
        return functools.reduce(jnp.add, parts)

    def values(v_ref, p, n):
        lane_g = lax.broadcasted_iota(jnp.int32, p.shape, 1) // rpk
        parts = [_dot_tn(v_ref[pl.ds(gk, n, stride=kv_heads), :].astype(BF16),
                         jnp.where(lane_g == gk, p, 0.0).astype(BF16)) for gk in range(kv_heads)]
        return functools.reduce(jnp.add, parts)

    @pl.when(g == 0)
    def _():
        _init_partials(m_scr, l_scr, g_scr)

    for bb in range(bps):
        j = g * bps + bb
        ss = [scores(k_refs[bb * ppb + t], PAGE_SIZE) for t in range(ppb)]
        g_scr[pl.ds(j, 1), :] = functools.reduce(jnp.add, [jnp.sum(s, axis=0, keepdims=True) for s in ss])
        m = functools.reduce(jnp.maximum, [jnp.max(s, axis=0, keepdims=True) for s in ss])
        ps = [jnp.exp(s - m) for s in ss]
        m_scr[pl.ds(j, 1), :] = m
        l_scr[pl.ds(j, 1), :] = functools.reduce(jnp.add, [jnp.sum(p, axis=0, keepdims=True) for p in ps])
        o_scr[j] = functools.reduce(jnp.add, [values(v_refs[bb * ppb + t], ps[t], PAGE_SIZE) for t in range(ppb)])

    @pl.when(g == pl.num_programs(1) - 1)
    def _():
        n_prev = o_scr.shape[0] - 1
        s = scores(kn_ref, n_new)
        tok = lax.broadcasted_iota(jnp.int32, s.shape, 0)
        qi = lax.broadcasted_iota(jnp.int32, s.shape, 1) % n_new
        s = jnp.where(tok <= qi, s, NEG_INF)
        m = jnp.max(s, axis=0, keepdims=True)
        p = jnp.exp(s - m)
        m_scr[n_prev:n_prev + 1, :] = m
        l_scr[n_prev:n_prev + 1, :] = jnp.sum(p, axis=0, keepdims=True)
        o_scr[n_prev] = values(vn_ref, p, n_new)
        w = _moba_weights(g_scr[...], m_scr[...], n_prev, n_prev, n_prev)
        den = jnp.sum(w * l_scr[...], axis=0, keepdims=True)
        acc = jnp.zeros((dh, rows), F32)
        for jj in range(n_prev + 1):
            acc = acc + w[jj:jj + 1, :] * o_scr[jj]
        o_ref[...] = (acc / den).T.astype(o_ref.dtype)


def moba_sample(qbd, k_new, v_new, cache_k, cache_v, page_table, layer, *, kv_heads, dh):
    n_seq, _, rows = qbd.shape
    n_new = k_new.shape[1] // kv_heads
    n_pages = page_table.shape[1]
    pages = PAGES_PER_STEP
    n_prev = n_pages * PAGE_SIZE // MB_BLOCK
    nbp = _round_up(n_prev + 1, SUBLANES)
    assert n_pages % pages == 0 and pages % (MB_BLOCK // PAGE_SIZE) == 0
    assert n_new <= MB_BLOCK

    def page_spec(t):
        return pl.BlockSpec((None, None, PAGE_SIZE * kv_heads, dh),
                            lambda b, g, pt: (layer, pt[b, g * pages + t], 0, 0))

    grid_spec = pltpu.PrefetchScalarGridSpec(
        num_scalar_prefetch=1,
        grid=(n_seq, n_pages // pages),
        in_specs=[pl.BlockSpec((None, kv_heads * dh, rows), lambda b, g, pt: (b, 0, 0)),
                  pl.BlockSpec((None, n_new * kv_heads, dh), lambda b, g, pt: (b, 0, 0)),
                  pl.BlockSpec((None, n_new * kv_heads, dh), lambda b, g, pt: (b, 0, 0))]
        + [page_spec(t) for t in range(pages)] * 2,
        out_specs=pl.BlockSpec((None, rows, dh), lambda b, g, pt: (b, 0, 0)),
        scratch_shapes=[pltpu.VMEM((nbp, rows), F32)] * 3 + [pltpu.VMEM((n_prev + 1, dh, rows), F32)],
    )
    return pl.pallas_call(
        functools.partial(_moba_sample_kernel, pages=pages, kv_heads=kv_heads, n_new=n_new),
        grid_spec=grid_spec,
        out_shape=jax.ShapeDtypeStruct((n_seq, rows, dh), F32),
        compiler_params=_cparams("parallel", "arbitrary"),
        name="moba_sample",
    )(page_table, qbd, k_new, v_new, *([cache_k] * pages), *([cache_v] * pages))


def _flash_init(m_scr, l_scr, acc_scr):
    m_scr[...] = jnp.full(m_scr.shape, NEG_INF, F32)
    l_scr[...] = jnp.zeros(l_scr.shape, F32)
    acc_scr[...] = jnp.zeros(acc_scr.shape, F32)


def _flash_update(s_list, pv, m_scr, l_scr, acc_scr):
    m_old = m_scr[...]
    m_new = functools.reduce(jnp.maximum, [jnp.max(s, axis=0, keepdims=True) for s in s_list] + [m_old])
    alpha = jnp.exp(m_old - m_new)
    ps = [jnp.exp(s - m_new) for s in s_list]
    l_scr[...] = alpha * l_scr[...] + functools.reduce(jnp.add, [jnp.sum(p, axis=0, keepdims=True) for p in ps])
    acc_scr[...] = alpha * acc_scr[...] + functools.reduce(
        jnp.add, [pv(idx, p.astype(BF16)) for idx, p in enumerate(ps)])
    m_scr[...] = m_new


def _mla_prompt_kernel(q_ref, qrt_ref, c_ref, ct_ref, kr_ref, wuk_ref, wuv_ref, o_ref, qlt_scr, m_scr, l_scr, acc_scr,
                       *, tq, tk, heads):
    i = pl.program_id(1)
    nope, vdim = wuk_ref.shape[2], wuv_ref.shape[2]
    for h in range(heads):
        qh = q_ref[:, h * nope:(h + 1) * nope].astype(BF16)
        qlt_scr[:, h * tq:(h + 1) * tq] = _dot_nt(wuk_ref[h], qh).astype(BF16)
    gw = m_scr.shape[1]
    hpg = gw // tq
    kidx = lax.broadcasted_iota(jnp.int32, (tk, gw), 0)
    qpos = i * tq + lax.broadcasted_iota(jnp.int32, (tk, gw), 1) % tq
    n_kb = ((i + 1) * tq + tk - 1) // tk
    for cg in range(heads // hpg):
        lanes = slice(cg * gw, (cg + 1) * gw)
        _flash_init(m_scr, l_scr, acc_scr)

        def body(j, carry):
            r0 = pl.multiple_of(j * tk, tk)
            s = _dot(c_ref[pl.ds(r0, tk), :], qlt_scr[:, lanes]) + _dot(kr_ref[pl.ds(r0, tk), :], qrt_ref[:, lanes])
            s = jnp.where(r0 + kidx <= qpos, s, NEG_INF)
            _flash_update([s], lambda idx, p: _dot(ct_ref[j], p), m_scr, l_scr, acc_scr)
            return carry

        lax.fori_loop(0, n_kb, body, 0)
        ot = (acc_scr[...] / l_scr[...]).astype(BF16)
        for hh in range(hpg):
            h = cg * hpg + hh
            o_ref[:, h * vdim:(h + 1) * vdim] = _dot_tn(ot[:, hh * tq:(hh + 1) * tq], wuv_ref[h]).astype(o_ref.dtype)


def mla_prompt(q, qrt, c, ct, kr, wuk, wuv, *, n_seq, seq_len):
    tq, tk = MLA_TQ, MLA_TK
    nq = seq_len // tq
    heads, cdim, nope = wuk.shape
    vdim = wuv.shape[2]
    rdim = kr.shape[1]
    return pl.pallas_call(
        functools.partial(_mla_prompt_kernel, tq=tq, tk=tk, heads=heads),
        grid=(n_seq, nq),
        in_specs=[pl.BlockSpec((tq, heads * nope), lambda b, i: (b * nq + i, 0)),
                  pl.BlockSpec((None, rdim, heads * tq), lambda b, i: (b * nq + i, 0, 0)),
                  pl.BlockSpec((seq_len, cdim), lambda b, i: (b, 0)),
                  pl.BlockSpec((None, seq_len // tk, cdim, tk), lambda b, i: (b, 0, 0, 0)),
                  pl.BlockSpec((seq_len, rdim), lambda b, i: (b, 0)),
                  pl.BlockSpec((heads, cdim, nope), lambda b, i: (0, 0, 0)),
                  pl.BlockSpec((heads, cdim, vdim), lambda b, i: (0, 0, 0))],
        out_specs=pl.BlockSpec((tq, heads * vdim), lambda b, i: (b * nq + i, 0)),
        out_shape=jax.ShapeDtypeStruct((n_seq * seq_len, heads * vdim), BF16),
        scratch_shapes=[pltpu.VMEM((cdim, heads * tq), BF16), pltpu.VMEM((1, MLA_GROUP_LANES), F32),
                        pltpu.VMEM((1, MLA_GROUP_LANES), F32), pltpu.VMEM((cdim, MLA_GROUP_LANES), F32)],
        compiler_params=_cparams("parallel", "arbitrary"),
        name="mla_prompt",
    )(q, qrt, c, ct, kr, wuk, wuv)


def _mla_sample_kernel(pt_ref, qlt_ref, qrt_ref, cn_ref, krn_ref, *refs, pages, heads):
    del pt_ref
    c_refs, krt_refs = refs[:pages], refs[pages:2 * pages]
    o_ref, m_scr, l_scr, acc_scr, call_scr, krall_scr = refs[2 * pages:]
    g = pl.program_id(1)

    @pl.when(g == 0)
    def _():
        _flash_init(m_scr, l_scr, acc_scr)

    qlt, qrt = qlt_ref[...], qrt_ref[...]
    n_chains = m_scr.shape[0]
    per = pages // n_chains
    for ch in range(n_chains):
        for t in range(per):
            call_scr[ch, t * PAGE_SIZE:(t + 1) * PAGE_SIZE, :] = c_refs[ch * per + t][...].astype(BF16)
            krall_scr[ch, :, t * PAGE_SIZE:(t + 1) * PAGE_SIZE] = krt_refs[ch * per + t][...].astype(BF16)
        c_all = call_scr[ch]
        s = _dot(c_all, qlt) + _dot_tn(krall_scr[ch], qrt)
        _flash_update([s], lambda idx, p, c_all=c_all: _dot_tn(c_all, p), m_scr.at[ch], l_scr.at[ch], acc_scr.at[ch])

    @pl.when(g == pl.num_programs(1) - 1)
    def _():
        cn = cn_ref[...].astype(BF16)
        s = _dot(cn, qlt) + _dot(krn_ref[...].astype(BF16), qrt)
        tok = lax.broadcasted_iota(jnp.int32, s.shape, 0)
        qi = lax.broadcasted_iota(jnp.int32, s.shape, 1) // heads
        s = jnp.where(tok <= qi, s, NEG_INF)
        _flash_update([s], lambda idx, p: _dot_tn(cn, p), m_scr.at[0], l_scr.at[0], acc_scr.at[0])
        m_all = functools.reduce(jnp.maximum, [m_scr[ch] for ch in range(n_chains)])
        ws = [jnp.exp(m_scr[ch] - m_all) for ch in range(n_chains)]
        l_all = functools.reduce(jnp.add, [ws[ch] * l_scr[ch] for ch in range(n_chains)])
        acc = functools.reduce(jnp.add, [ws[ch] * acc_scr[ch] for ch in range(n_chains)])
        o_ref[...] = (acc / l_all).T.astype(o_ref.dtype)


def mla_sample(qlt, qrt, c_new, kr_new, cache_c, cache_krt, page_table, layer, *, heads):
    n_seq, n_pages = page_table.shape
    pages = PAGES_PER_STEP
    n_new = c_new.shape[0] // n_seq
    rows = n_new * heads
    cdim, rdim = c_new.shape[1], kr_new.shape[1]
    assert n_pages % pages == 0

    def c_spec(t):
        return pl.BlockSpec((None, None, PAGE_SIZE, cdim), lambda b, g, pt: (layer, pt[b, g * pages + t], 0, 0))

    def kr_spec(t):
        return pl.BlockSpec((None, None, rdim, PAGE_SIZE), lambda b, g, pt: (layer, pt[b, g * pages + t], 0, 0))

    grid_spec = pltpu.PrefetchScalarGridSpec(
        num_scalar_prefetch=1,
        grid=(n_seq, n_pages // pages),
        in_specs=[pl.BlockSpec((None, cdim, rows), lambda b, g, pt: (b, 0, 0)),
                  pl.BlockSpec((None, rdim, rows), lambda b, g, pt: (b, 0, 0)),
                  pl.BlockSpec((n_new, cdim), lambda b, g, pt: (b, 0)),
                  pl.BlockSpec((n_new, rdim), lambda b, g, pt: (b, 0))]
        + [c_spec(t) for t in range(pages)] + [kr_spec(t) for t in range(pages)],
        out_specs=pl.BlockSpec((rows, cdim), lambda b, g, pt: (b, 0)),
        scratch_shapes=[pltpu.VMEM((MLA_DECODE_CHAINS, 1, rows), F32), pltpu.VMEM((MLA_DECODE_CHAINS, 1, rows), F32),
                        pltpu.VMEM((MLA_DECODE_CHAINS, cdim, rows), F32),
                        pltpu.VMEM((MLA_DECODE_CHAINS, pages // MLA_DECODE_CHAINS * PAGE_SIZE, cdim), BF16),
                        pltpu.VMEM((MLA_DECODE_CHAINS, rdim, pages // MLA_DECODE_CHAINS * PAGE_SIZE), BF16)],
    )
    return pl.pallas_call(
        functools.partial(_mla_sample_kernel, pages=pages, heads=heads),
        grid_spec=grid_spec,
        out_shape=jax.ShapeDtypeStruct((n_seq * rows, cdim), BF16),
        compiler_params=_cparams("parallel", "arbitrary"),
        name="mla_sample",
    )(page_table, qlt, qrt, c_new, kr_new, *([cache_c] * pages), *([cache_krt] * pages))


TM = 1024
MIXER_OUT_TK = 2048
FFN_OUT_K_SPLITS = 4


def _hgrn_layer(x, norm_g, w_in, w_out, layer, lb, gnorm, s0, *, n_prompt, prompt_len, n_sample, sample_len):
    d = x.shape[1]
    dk = d // HG_HEADS
    p = norm_matmul(x, norm_g, w_in, layer, tm=TM, tn=512)
    kw = dict(mode="hgrn", heads=HG_HEADS, dk=dk, dv=dk, scale=dk ** -0.5)
    op, sp = lin_attn(p, (lb,), gnorm, None, n_seq=n_prompt, seq_len=prompt_len, row0=0, **kw)
    os_, ss = lin_attn(p, (lb,), gnorm, s0, n_seq=n_sample, seq_len=sample_len, row0=n_prompt * prompt_len, **kw)
    o = jnp.concatenate([op, os_], axis=0)
    return matmul_res(o, w_out, layer, x, tm=TM, tn=1024, tk=MIXER_OUT_TK), sp, ss


def _gla_layer(x, norm_g, w_in, w_gate_up, b_gate, gnorm, w_out, layer, s0, *, n_prompt, prompt_len, n_sample,
               sample_len):
    d = x.shape[1]
    dk, dv = d // 2 // GLA_HEADS, d // GLA_HEADS
    w_pad = jnp.pad(w_in, ((0, 0), (0, LANES - GLA_GATE_RANK))).astype(BF16)[None]
    p = norm_matmul(x, norm_g, w_pad, 0, tm=TM, tn=w_pad.shape[2] // 7)
    wg = jnp.pad(w_gate_up, ((0, LANES - GLA_GATE_RANK), (0, 0))).astype(BF16)
    kw = dict(mode="gla", heads=GLA_HEADS, dk=dk, dv=dv, scale=dk ** -0.5)
    op, sp = lin_attn(p, (wg, b_gate), gnorm, None, n_seq=n_prompt, seq_len=prompt_len, row0=0, **kw)
    os_, ss = lin_attn(p, (wg, b_gate), gnorm, s0, n_seq=n_sample, seq_len=sample_len,
                       row0=n_prompt * prompt_len, **kw)
    o = jnp.concatenate([op, os_], axis=0)
    return matmul_res(o, w_out, layer, x, tm=TM, tn=1024, tk=MIXER_OUT_TK), sp, ss


def _moba_layer(x, norm_g, w_in, w_out, layer, cache_k, cache_v, page_table, *, n_prompt, prompt_len, n_sample,
                sample_len):
    dh, h, hkv = MB_HEAD_DIM, MB_HEADS, MB_KV_HEADS
    tp = n_prompt * prompt_len
    p = norm_matmul(x, norm_g, w_in, layer, tm=TM, tn=512)
    k = p[:, h * dh:(h + hkv) * dh]
    v = p[:, (h + hkv) * dh:]
    ap = moba_prompt(p, n_seq=n_prompt, seq_len=prompt_len, heads=h, kv_heads=hkv, dh=dh)
    qt = (p[tp:, :h * dh] * dh ** -0.5).reshape(n_sample, sample_len, h, dh).transpose(0, 3, 2, 1)
    qt = qt.reshape(n_sample, dh, h * sample_len)
    kv_of_col = jnp.arange(h * sample_len) // (h // hkv * sample_len)
    qbd = jnp.where(kv_of_col[None, None, None, :] == jnp.arange(hkv)[None, :, None, None], qt[:, None], 0.0)
    qbd = qbd.reshape(n_sample, hkv * dh, h * sample_len).astype(BF16)
    ks = k[tp:].reshape(n_sample, sample_len * hkv, dh)
    vs = v[tp:].reshape(n_sample, sample_len * hkv, dh)
    n_layers, n_pool = cache_k.shape[:2]
    ck = cache_k.reshape(n_layers, n_pool, PAGE_SIZE * hkv, dh)
    cv = cache_v.reshape(n_layers, n_pool, PAGE_SIZE * hkv, dh)
    as_ = moba_sample(qbd, ks, vs, ck, cv, page_table, layer, kv_heads=hkv, dh=dh)
    as_ = as_.reshape(n_sample, h, sample_len, dh).transpose(0, 2, 1, 3).reshape(n_sample * sample_len, h * dh)
    a = jnp.concatenate([ap, as_.astype(BF16)], axis=0)
    x = matmul_res(a, w_out, layer, x, tm=TM, tn=1024, tk=MIXER_OUT_TK)
    shape_p = (n_prompt, prompt_len, hkv, dh)
    shape_s = (n_sample, sample_len, hkv, dh)
    return x, k[:tp].reshape(shape_p), v[:tp].reshape(shape_p), k[tp:].reshape(shape_s), v[tp:].reshape(shape_s)


def _rope_tables(pos):
    half = MLA_ROPE // 2
    inv = ROPE_THETA ** (-jnp.arange(half, dtype=F32) / half)
    ang = pos.astype(F32)[:, None] * inv[None, :]
    reps = LANES // half
    return jnp.tile(jnp.cos(ang), (1, reps)), jnp.tile(jnp.sin(ang), (1, reps))


def _mla_layer(x, norm_g, w_in, q_norm, w_q_up, kv_norm, w_kv_up, w_out, layer, cache_c, cache_kr, page_table, pos, *,
               n_prompt, prompt_len, n_sample, sample_len):
    t = x.shape[0]
    h, half = MLA_HEADS, MLA_ROPE // 2
    tp = n_prompt * prompt_len
    d = w_in.shape[0]
    scale = (MLA_NOPE + MLA_ROPE) ** -0.5
    qa_w, ckv_w, kr_w = w_in[:, :MLA_Q_RANK], w_in[:, MLA_Q_RANK:MLA_Q_RANK + MLA_KV_RANK], w_in[:, MLA_Q_RANK + MLA_KV_RANK:]
    qpad = 1024 - MLA_Q_RANK
    zeros = lambda n: jnp.zeros((d, n), w_in.dtype)
    w_p = jnp.concatenate([qa_w, zeros(qpad), ckv_w, kr_w[:, :half], zeros(LANES - half), kr_w[:, half:],
                           zeros(LANES - half)], axis=1).astype(BF16)[None]
    p = norm_matmul(x, norm_g, w_p, 0, tm=TM, tn=256)
    w4 = w_q_up.reshape(MLA_Q_RANK, h, MLA_NOPE + MLA_ROPE)
    w_q = jnp.concatenate([w4[:, :, :MLA_NOPE].reshape(MLA_Q_RANK, h * MLA_NOPE),
                           w4[:, :, MLA_NOPE:MLA_NOPE + half].reshape(MLA_Q_RANK, h * half),
                           w4[:, :, MLA_NOPE + half:].reshape(MLA_Q_RANK, h * half)], axis=1).astype(BF16)[None]
    q3 = norm_matmul(p, q_norm, w_q, 0, tm=TM, tn=512)
    c32 = rmsnorm_rows(p, kv_norm, col=1024 // MLA_KV_RANK, tm=TM)
    cos, sin = _rope_tables(pos)
    rc = h * MLA_NOPE // (h * half)
    qr1, qr2 = rope_pairs(q3, cos, sin, col1=rc, col2=rc + 1, width=h * half, tm=TM)
    kc = (1024 + MLA_KV_RANK) // LANES
    kr1, kr2 = rope_pairs(p, cos, sin, col1=kc, col2=kc + 1, width=LANES, tm=TM)
    kr = jnp.concatenate([kr1[:, :half], kr2[:, :half]], axis=1)
    q_rope = (jnp.concatenate([qr1.reshape(t, h, half), qr2.reshape(t, h, half)], axis=-1) * scale).astype(BF16)
    wkv = w_kv_up.reshape(MLA_KV_RANK, h, MLA_NOPE + MLA_V)
    w_uk = wkv[:, :, :MLA_NOPE] * scale
    w_uv = wkv[:, :, MLA_NOPE:].transpose(1, 0, 2).astype(BF16)
    c16, kr16 = c32.astype(BF16), kr.astype(BF16)
    tq, tk = MLA_TQ, MLA_TK
    qrt_p = q_rope[:tp].reshape(tp // tq, tq, h, MLA_ROPE).transpose(0, 3, 2, 1).reshape(tp // tq, MLA_ROPE, h * tq)
    ct = c16[:tp].reshape(n_prompt, prompt_len // tk, tk, MLA_KV_RANK).transpose(0, 1, 3, 2)
    op = mla_prompt(q3, qrt_p, c16, ct, kr16, w_uk.transpose(1, 0, 2).astype(BF16), w_uv,
                    n_seq=n_prompt, seq_len=prompt_len)
    ql_s = head_matmul(q3[tp:], w_uk.transpose(1, 2, 0).astype(BF16), tm=n_sample * sample_len)
    rows = sample_len * h
    qlt_s = ql_s.reshape(n_sample, rows, MLA_KV_RANK).transpose(0, 2, 1)
    qrt_s = q_rope[tp:].reshape(n_sample, rows, MLA_ROPE).transpose(0, 2, 1)
    os_ = mla_sample(qlt_s, qrt_s, c32[tp:], kr[tp:], cache_c, jnp.swapaxes(cache_kr, 2, 3), page_table, layer, heads=h)
    os_ = head_matmul(os_.reshape(n_sample * sample_len, h * MLA_KV_RANK), w_uv, tm=n_sample * sample_len)
    o = jnp.concatenate([op, os_], axis=0)
    x = matmul_res(o, w_out, layer, x, tm=TM, tn=1024, tk=MIXER_OUT_TK)
    return (x, c32[:tp].reshape(n_prompt, prompt_len, -1), kr[:tp].reshape(n_prompt, prompt_len, -1),
            c32[tp:].reshape(n_sample, sample_len, -1), kr[tp:].reshape(n_sample, sample_len, -1))


def _dense_ffn(x, norm_g, w_in, w_out, layer):
    act = swiglu_proj(x, norm_g, w_in[:, None], layer, None, tm=TM, tn=512)
    return matmul_res(act, w_out, layer, x, tm=TM, tn=1024, tk=w_out.shape[1] // FFN_OUT_K_SPLITS)


def _moe_ffn(x, norm_g, wr, w_in, w_out, layer):
    rw, h = router(x, norm_g, wr, tm=512)
    n_experts = wr.shape[1]
    plan = _moe_plan(rw, n_experts)
    xs, row_w = moe_gather(h, plan)
    act = moe_up(xs, row_w, w_in, layer, plan)
    y = moe_down(act, w_out, layer, plan)
    return moe_combine(x, y, plan, n_experts)


def kernel(x_prompt, x_sample, state_hgrn, state_gla, cache_moba_k, cache_moba_v, cache_mla_latent, cache_mla_krope,
           page_table, norm_mixer, norm_ffn, norm_final, hgrn_w_in, hgrn_lb_logits, hgrn_gnorm, hgrn_w_out, gla_w_in,
           gla_w_gate_up, gla_b_gate, gla_gnorm, gla_w_out, moba_w_in, moba_w_out, mla_w_in, mla_q_norm, mla_w_q_up,
           mla_kv_norm, mla_w_kv_up, mla_w_out, ffn_w_in, ffn_w_out, moe_router, moe_w_in, moe_w_out):
    n_prompt, prompt_len, d = x_prompt.shape
    n_sample, sample_len, _ = x_sample.shape
    past_len = page_table.shape[1] * PAGE_SIZE
    depth = norm_mixer.shape[0]
    dims = dict(n_prompt=n_prompt, prompt_len=prompt_len, n_sample=n_sample, sample_len=sample_len)
    tp = n_prompt * prompt_len
    pos = jnp.concatenate([jnp.tile(jnp.arange(prompt_len, dtype=jnp.int32), n_prompt),
                           jnp.tile(past_len + jnp.arange(sample_len, dtype=jnp.int32), n_sample)])
    lb_all = jnp.cumsum(jax.nn.softmax(hgrn_lb_logits.astype(F32), axis=0), axis=0)
    x = jnp.concatenate([x_prompt.reshape(tp, d), x_sample.reshape(-1, d)], axis=0)
    bf = lambda w: w.astype(BF16)
    hgrn_w_in, hgrn_w_out, gla_w_out, moba_w_in, moba_w_out, mla_w_out = (
        bf(hgrn_w_in), bf(hgrn_w_out), bf(gla_w_out), bf(moba_w_in), bf(moba_w_out), bf(mla_w_out))
    ffn_w_in, ffn_w_out, moe_w_in, moe_w_out = bf(ffn_w_in), bf(ffn_w_out), bf(moe_w_in), bf(moe_w_out)
    outs = {k: [] for k in ("hg_p", "hg_s", "gla_p", "gla_s", "mbk_p", "mbv_p", "mbk_s", "mbv_s",
                            "mlc_p", "mlr_p", "mlc_s", "mlr_s")}
    for i in range(depth):
        m, j = i % 4, i // 4
        if m == 0:
            x, sp, ss = _hgrn_layer(x, norm_mixer[i], hgrn_w_in, hgrn_w_out, j, lb_all[i], hgrn_gnorm[j],
                                    state_hgrn[j], **dims)
            outs["hg_p"].append(sp)
            outs["hg_s"].append(ss)
        elif m == 1:
            x, sp, ss = _gla_layer(x, norm_mixer[i], gla_w_in[j], gla_w_gate_up[j], gla_b_gate[j], gla_gnorm[j],
                                   gla_w_out, j, state_gla[j], **dims)
            outs["gla_p"].append(sp)
            outs["gla_s"].append(ss)
        elif m == 2:
            x, kp, vp, ks, vs = _moba_layer(x, norm_mixer[i], moba_w_in, moba_w_out, j, cache_moba_k, cache_moba_v,
                                            page_table, **dims)
            for key, val in zip(("mbk_p", "mbv_p", "mbk_s", "mbv_s"), (kp, vp, ks, vs)):
                outs[key].append(val)
        else:
            x, cp, rp, cs, rs = _mla_layer(x, norm_mixer[i], mla_w_in[j], mla_q_norm[j], mla_w_q_up[j], mla_kv_norm[j],
                                           mla_w_kv_up[j], mla_w_out, j, cache_mla_latent, cache_mla_krope, page_table,
                                           pos, **dims)
            for key, val in zip(("mlc_p", "mlr_p", "mlc_s", "mlr_s"), (cp, rp, cs, rs)):
                outs[key].append(val)
        if i % 2 == 0:
            x = _dense_ffn(x, norm_ffn[i], ffn_w_in, ffn_w_out, i // 2)
        else:
            x = _moe_ffn(x, norm_ffn[i], moe_router[i // 2], moe_w_in, moe_w_out, i // 2)
    y = rmsnorm_rows(x, norm_final, tm=TM)
    return (y[:tp].reshape(x_prompt.shape), y[tp:].reshape(x_sample.shape),
            *(jnp.stack(outs[k]) for k in ("hg_p", "hg_s", "gla_p", "gla_s", "mbk_p", "mbv_p", "mbk_s", "mbv_s",
                                           "mlc_p", "mlr_p", "mlc_s", "mlr_s")))
```

```python
import functools

import jax
import jax.numpy as jnp
from jax import lax
from jax.experimental import pallas as pl
from jax.experimental.pallas import tpu as pltpu

F32 = jnp.float32
BF16 = jnp.bfloat16
NEG_INF = float("-inf")

EPS = 1e-6
PAGE_SIZE = 128
HG_HEADS = 16
GLA_HEADS = 4
GLA_GATE_RANK = 16
GLA_GATE_NORM = 16.0
MB_HEADS = 16
MB_KV_HEADS = 4
MB_HEAD_DIM = 128
MB_BLOCK = 256
MB_TOPK = 3
MLA_HEADS = 16
MLA_Q_RANK = 768
MLA_KV_RANK = 512
MLA_NOPE = 128
MLA_ROPE = 64
MLA_V = 128
ROPE_THETA = 10000.0

LANES = 128
SUBLANES = 8
VMEM_LIMIT = 48 * 2 ** 20
LA_CHUNK = 64
PAGES_PER_STEP = 16
STATE_BLOCK_BYTES = 4 * 2 ** 20
MLA_TQ = 128
MLA_TK = 512
MLA_GROUP_LANES = 1024
MLA_DECODE_CHAINS = 2
MOE_TILE = 256
MOE_CHUNK = 1024
MOE_UP_SPLITS = 2
MOE_COMBINE_GROUP = 4


def _cparams(*sem):
    return pltpu.CompilerParams(dimension_semantics=sem, vmem_limit_bytes=VMEM_LIMIT)


def _dot(a, b):
    return jnp.dot(a, b, preferred_element_type=F32)


def _dot_nt(a, b):
    return lax.dot_general(a, b, (((1,), (1,)), ((), ())), preferred_element_type=F32)


def _dot_tn(a, b):
    return lax.dot_general(a, b, (((0,), (0,)), ((), ())), preferred_element_type=F32)


def _norm_rows(x, g):
    return x * lax.rsqrt(jnp.mean(x * x, axis=-1, keepdims=True) + EPS) * g


def _lane_col(x, j):
    lane = lax.broadcasted_iota(jnp.int32, x.shape, 1)
    return jnp.sum(jnp.where(lane == j, x, 0.0), axis=-1, keepdims=True)


def _round_up(n, m):
    return -(-n // m) * m


def _norm_matmul_kernel(x_ref, g_ref, w_ref, o_ref, h_ref):
    @pl.when(pl.program_id(1) == 0)
    def _():
        h_ref[...] = _norm_rows(x_ref[...], g_ref[...]).astype(BF16)

    o_ref[...] = _dot(h_ref[...], w_ref[...]).astype(o_ref.dtype)


def norm_matmul(x, g, w, layer, *, tm, tn, out_dtype=F32):
    m = x.shape[0]
    _, k, n = w.shape
    return pl.pallas_call(
        _norm_matmul_kernel,
        grid=(m // tm, n // tn),
        in_specs=[pl.BlockSpec((tm, k), lambda i, j: (i, 0)),
                  pl.BlockSpec((1, k), lambda i, j: (0, 0)),
                  pl.BlockSpec((None, k, tn), lambda i, j: (layer, 0, j))],
        out_specs=pl.BlockSpec((tm, tn), lambda i, j: (i, j)),
        out_shape=jax.ShapeDtypeStruct((m, n), out_dtype),
        scratch_shapes=[pltpu.VMEM((tm, k), BF16)],
        compiler_params=_cparams("parallel", "arbitrary"),
        name="norm_matmul",
    )(x, g.reshape(1, k).astype(F32), w)


def _swiglu_kernel(*refs, has_norm, has_scale, nf):
    refs = list(refs)
    x_ref = refs.pop(0)
    g_ref = refs.pop(0) if has_norm else None
    wg_ref, wu_ref = refs.pop(0), refs.pop(0)
    rw_ref = refs.pop(0) if has_scale else None
    o_ref = refs.pop(0)
    if has_norm:
        h_ref = refs.pop(0)

        @pl.when(pl.program_id(1) == 0)
        def _():
            h_ref[...] = _norm_rows(x_ref[...], g_ref[...]).astype(BF16)

        h = h_ref[...]
    else:
        h = x_ref[...]
    a = _dot(h, wg_ref[...])
    u = _dot(h, wu_ref[...])
    act = a * jax.nn.sigmoid(a) * u
    if has_scale:
        act = act * _lane_col(rw_ref[...], pl.program_id(1) // nf)
    o_ref[...] = act.astype(o_ref.dtype)


def swiglu_proj(x, g, w, layer, rw, *, tm, tn):
    m, k = x.shape
    _, e, _, f2 = w.shape
    f = f2 // 2
    nf = f // tn
    has_norm, has_scale = g is not None, rw is not None
    in_specs = [pl.BlockSpec((tm, k), lambda i, j: (i, 0))]
    args = [x]
    if has_norm:
        in_specs.append(pl.BlockSpec((1, k), lambda i, j: (0, 0)))
        args.append(g.reshape(1, k).astype(F32))
    in_specs += [pl.BlockSpec((None, None, k, tn), lambda i, j: (layer, j // nf, 0, j % nf)),
                 pl.BlockSpec((None, None, k, tn), lambda i, j: (layer, j // nf, 0, nf + j % nf))]
    args += [w, w]
    if has_scale:
        in_specs.append(pl.BlockSpec((tm, LANES), lambda i, j: (i, 0)))
        args.append(rw)
    return pl.pallas_call(
        functools.partial(_swiglu_kernel, has_norm=has_norm, has_scale=has_scale, nf=nf),
        grid=(m // tm, e * nf),
        in_specs=in_specs,
        out_specs=pl.BlockSpec((tm, tn), lambda i, j: (i, j)),
        out_shape=jax.ShapeDtypeStruct((m, e * f), BF16),
        scratch_shapes=[pltpu.VMEM((tm, k), BF16)] if has_norm else [],
        compiler_params=_cparams("parallel", "arbitrary"),
        name="swiglu_proj",
    )(*args)


def _matmul_res_kernel(a_ref, w_ref, r_ref, o_ref):
    @pl.when(pl.program_id(2) == 0)
    def _():
        o_ref[...] = r_ref[...]

    o_ref[...] += _dot(a_ref[...], w_ref[...])


def matmul_res(a, w, layer, res, *, tm, tn, tk):
    m, k = a.shape
    n = w.shape[2]
    return pl.pallas_call(
        _matmul_res_kernel,
        grid=(m // tm, n // tn, k // tk),
        in_specs=[pl.BlockSpec((tm, tk), lambda i, j, kk: (i, kk)),
                  pl.BlockSpec((None, tk, tn), lambda i, j, kk: (layer, kk, j)),
                  pl.BlockSpec((tm, tn), lambda i, j, kk: (i, j))],
        out_specs=pl.BlockSpec((tm, tn), lambda i, j, kk: (i, j)),
        out_shape=jax.ShapeDtypeStruct((m, n), F32),
        compiler_params=_cparams("parallel", "parallel", "arbitrary"),
        name="matmul_res",
    )(a, w, res)


def _rmsnorm_kernel(x_ref, g_ref, o_ref):
    o_ref[...] = _norm_rows(x_ref[...], g_ref[...]).astype(o_ref.dtype)


def rmsnorm_rows(x, g, *, col=0, tm, out_dtype=F32):
    m = x.shape[0]
    k = g.shape[-1]
    return pl.pallas_call(
        _rmsnorm_kernel,
        grid=(m // tm,),
        in_specs=[pl.BlockSpec((tm, k), lambda i: (i, col)),
                  pl.BlockSpec((1, k), lambda i: (0, 0))],
        out_specs=pl.BlockSpec((tm, k), lambda i: (i, 0)),
        out_shape=jax.ShapeDtypeStruct((m, k), out_dtype),
        compiler_params=_cparams("parallel"),
        name="rmsnorm_rows",
    )(x, g.reshape(1, k).astype(F32))


def _router_kernel(x_ref, g_ref, wr_ref, rw_ref, h_ref, *, n_experts):
    h = _norm_rows(x_ref[...], g_ref[...])
    h_ref[...] = h.astype(BF16)
    logits = jnp.dot(h, wr_ref[...], preferred_element_type=F32, precision=lax.Precision.HIGHEST)
    lane = lax.broadcasted_iota(jnp.int32, logits.shape, 1)
    l1 = jnp.where(lane < n_experts, logits, NEG_INF)
    m1 = jnp.max(l1, axis=-1, keepdims=True)
    i1 = jnp.min(jnp.where(l1 == m1, lane, LANES), axis=-1, keepdims=True)
    l2 = jnp.where(lane == i1, NEG_INF, l1)
    m2 = jnp.max(l2, axis=-1, keepdims=True)
    i2 = jnp.min(jnp.where(l2 == m2, lane, LANES), axis=-1, keepdims=True)
    e2 = jnp.exp(m2 - m1)
    den = 1.0 + e2
    rw_ref[...] = jnp.where(lane == i1, 1.0 / den, 0.0) + jnp.where(lane == i2, e2 / den, 0.0)


def router(x, g, wr, *, tm):
    m, k = x.shape
    n_experts = wr.shape[1]
    wr_pad = jnp.pad(wr.astype(F32), ((0, 0), (0, LANES - n_experts)))
    return pl.pallas_call(
        functools.partial(_router_kernel, n_experts=n_experts),
        grid=(m // tm,),
        in_specs=[pl.BlockSpec((tm, k), lambda i: (i, 0)),
                  pl.BlockSpec((1, k), lambda i: (0, 0)),
                  pl.BlockSpec((k, LANES), lambda i: (0, 0))],
        out_specs=[pl.BlockSpec((tm, LANES), lambda i: (i, 0)),
                   pl.BlockSpec((tm, k), lambda i: (i, 0))],
        out_shape=[jax.ShapeDtypeStruct((m, LANES), F32), jax.ShapeDtypeStruct((m, k), BF16)],
        compiler_params=_cparams("parallel"),
        name="router",
    )(x, g.reshape(1, k).astype(F32), wr_pad)


def _moe_plan(rw, n_experts):
    t = rw.shape[0]
    tile, chunk = MOE_TILE, MOE_CHUNK
    n_tiles = 2 * t // tile + n_experts
    n_rows = n_tiles * tile
    w = rw[:, :n_experts]
    mask = w > 0
    cnt = jnp.sum(mask, axis=0, dtype=jnp.int32)
    pos = jnp.cumsum(mask, axis=0, dtype=jnp.int32) - 1
    tiles_e = (cnt + tile - 1) // tile
    ends = jnp.cumsum(tiles_e)
    dest = jnp.where(mask, ((ends - tiles_e) * tile)[None, :] + pos, -1)
    tile_ids = jnp.arange(n_tiles, dtype=jnp.int32)
    tile_expert = jnp.minimum(jnp.sum(ends[None, :] <= tile_ids[:, None], axis=1), n_experts - 1).astype(jnp.int32)
    dc = dest.reshape(t // chunk, chunk, n_experts)
    c_hi_tile = jnp.max(dc, axis=1) // tile
    c_lo_tile = jnp.min(jnp.where(dc >= 0, dc, n_rows), axis=1) // tile
    lo_sel = jnp.take(c_lo_tile.T, tile_expert, axis=0)
    hi_sel = jnp.take(c_hi_tile.T, tile_expert, axis=0)
    covers = (hi_sel >= 0) & (lo_sel <= tile_ids[:, None]) & (tile_ids[:, None] <= hi_sel)
    cidx = jnp.arange(t // chunk, dtype=jnp.int32)[None, :]
    first = jnp.min(jnp.where(covers, cidx, t // chunk), axis=1)
    last = jnp.max(jnp.where(covers, cidx, -1), axis=1)
    c_lo = jnp.where(last >= 0, first, 0).astype(jnp.int32)
    n_c = jnp.where(last >= 0, last - first + 1, 0).astype(jnp.int32)
    d3 = dest.reshape(t // tile, tile, n_experts)
    lo = jnp.min(jnp.where(d3 >= 0, d3, n_rows), axis=1)
    hi = jnp.max(d3, axis=1)
    b0 = jnp.where(hi >= 0, lo // tile, 0).astype(jnp.int32).reshape(-1)
    nb = jnp.where(hi >= 0, hi // tile - lo // tile + 1, 0).astype(jnp.int32).reshape(-1)
    dest_f = dest.astype(F32)
    pad = _round_up(n_experts, SUBLANES) - n_experts
    return dict(n_tiles=n_tiles, tile_expert=tile_expert, n_used=ends[-1:].astype(jnp.int32), c_lo=c_lo, n_c=n_c,
                b0=b0, nb=nb,
                dest=jnp.pad(dest_f, ((0, 0), (0, LANES - n_experts)), constant_values=-1.0),
                dest_t=jnp.pad(dest_f.T, ((0, pad), (0, 0)), constant_values=-1.0),
                w_t=jnp.pad(w.T, ((0, pad), (0, 0))))


def _moe_gather_kernel(te_ref, clo_ref, nc_ref, dest_ref, w_ref, h_ref, o_ref, rw_ref, acc_scr, rw_scr):
    r, c = pl.program_id(0), pl.program_id(1)

    @pl.when(c == 0)
    def _():
        acc_scr[...] = jnp.zeros_like(acc_scr)
        rw_scr[...] = jnp.zeros_like(rw_scr)

    @pl.when(c < nc_ref[r])
    def _():
        tile = o_ref.shape[0]
        e = te_ref[r]
        rel = dest_ref[pl.ds(e, 1), :] - (r * tile).astype(F32)
        row = lax.broadcasted_iota(jnp.int32, (tile, rel.shape[1]), 0).astype(F32)
        hit = rel == row
        acc_scr[...] += _dot(jnp.where(hit, 1.0, 0.0).astype(BF16), h_ref[...])
        rw_scr[...] += jnp.sum(jnp.where(hit, w_ref[pl.ds(e, 1), :], 0.0), axis=-1, keepdims=True)

    @pl.when(c == pl.num_programs(1) - 1)
    def _():
        o_ref[...] = acc_scr[...].astype(o_ref.dtype)
        rw_ref[...] = rw_scr[...]


def moe_gather(h, plan):
    t, k = h.shape
    tile, chunk = MOE_TILE, MOE_CHUNK
    n_tiles = plan["n_tiles"]
    ne = plan["dest_t"].shape[0]

    def chunk_of(r, c, te, clo, nc):
        return clo[r] + jnp.minimum(c, jnp.maximum(nc[r], 1) - 1)

    grid_spec = pltpu.PrefetchScalarGridSpec(
        num_scalar_prefetch=3,
        grid=(n_tiles, t // chunk),
        in_specs=[pl.BlockSpec((ne, chunk), lambda *a: (0, chunk_of(*a))),
                  pl.BlockSpec((ne, chunk), lambda *a: (0, chunk_of(*a))),
                  pl.BlockSpec((chunk, k), lambda *a: (chunk_of(*a), 0))],
        out_specs=[pl.BlockSpec((tile, k), lambda r, c, te, clo, nc: (r, 0)),
                   pl.BlockSpec((tile, 1), lambda r, c, te, clo, nc: (r, 0))],
        scratch_shapes=[pltpu.VMEM((tile, k), F32), pltpu.VMEM((tile, 1), F32)],
    )
    return pl.pallas_call(
        _moe_gather_kernel,
        grid_spec=grid_spec,
        out_shape=[jax.ShapeDtypeStruct((n_tiles * tile, k), BF16), jax.ShapeDtypeStruct((n_tiles * tile, 1), F32)],
        compiler_params=_cparams("parallel", "arbitrary"),
        name="moe_gather",
    )(plan["tile_expert"], plan["c_lo"], plan["n_c"], plan["dest_t"], plan["w_t"], h)


def _moe_up_kernel(te_ref, nu_ref, x_ref, wg_ref, wu_ref, rw_ref, o_ref):
    del te_ref
    used = pl.program_id(1) < nu_ref[0]

    @pl.when(used)
    def _():
        h = x_ref[...]
        a = _dot(h, wg_ref[...])
        u = _dot(h, wu_ref[...])
        o_ref[...] = (a * jax.nn.sigmoid(a) * u * rw_ref[...]).astype(o_ref.dtype)

    @pl.when(jnp.logical_not(used))
    def _():
        o_ref[...] = jnp.zeros_like(o_ref)


def moe_up(xs, row_w, w, layer, plan):
    n_rows, k = xs.shape
    f = w.shape[3] // 2
    tile = MOE_TILE
    nf = MOE_UP_SPLITS
    tn = f // nf
    grid_spec = pltpu.PrefetchScalarGridSpec(
        num_scalar_prefetch=2,
        grid=(nf, n_rows // tile),
        in_specs=[pl.BlockSpec((tile, k), lambda j, r, te, nu: (r, 0)),
                  pl.BlockSpec((None, None, k, tn), lambda j, r, te, nu: (layer, te[r], 0, j)),
                  pl.BlockSpec((None, None, k, tn), lambda j, r, te, nu: (layer, te[r], 0, nf + j)),
                  pl.BlockSpec((tile, 1), lambda j, r, te, nu: (r, 0))],
        out_specs=pl.BlockSpec((tile, tn), lambda j, r, te, nu: (r, j)),
    )
    return pl.pallas_call(
        _moe_up_kernel,
        grid_spec=grid_spec,
        out_shape=jax.ShapeDtypeStruct((n_rows, f), BF16),
        compiler_params=_cparams("arbitrary", "arbitrary"),
        name="moe_up",
    )(plan["tile_expert"], plan["n_used"], xs, w, w, row_w)


def _moe_down_kernel(te_ref, nu_ref, a_ref, w_ref, o_ref):
    del te_ref
    used = pl.program_id(0) < nu_ref[0]

    @pl.when(used)
    def _():
        o_ref[...] = _dot(a_ref[...], w_ref[...]).astype(o_ref.dtype)

    @pl.when(jnp.logical_not(used))
    def _():
        o_ref[...] = jnp.zeros_like(o_ref)


def moe_down(act, w, layer, plan):
    n_rows, f = act.shape
    d = w.shape[3]
    tile = MOE_TILE
    grid_spec = pltpu.PrefetchScalarGridSpec(
        num_scalar_prefetch=2,
        grid=(n_rows // tile,),
        in_specs=[pl.BlockSpec((tile, f), lambda r, te, nu: (r, 0)),
                  pl.BlockSpec((None, None, f, d), lambda r, te, nu: (layer, te[r], 0, 0))],
        out_specs=pl.BlockSpec((tile, d), lambda r, te, nu: (r, 0)),
    )
    return pl.pallas_call(
        _moe_down_kernel,
        grid_spec=grid_spec,
        out_shape=jax.ShapeDtypeStruct((n_rows, d), BF16),
        compiler_params=_cparams("arbitrary"),
        name="moe_down",
    )(plan["tile_expert"], plan["n_used"], act, w)


def _moe_combine_kernel(b0_ref, nb_ref, dest_ref, x_ref, *refs, n_experts, group):
    y_refs, o_ref = refs[:-1], refs[-1]
    tt, eg = pl.program_id(0), pl.program_id(1)
    tile = o_ref.shape[0]
    lane = lax.broadcasted_iota(jnp.int32, (tile, tile), 1).astype(F32)
    dest = dest_ref[...]
    parts = []
    for ee in range(group):
        e = eg * group + ee
        col = _lane_col(dest, e)
        for kk in range(2):
            rel = col - ((b0_ref[tt * n_experts + e] + kk) * tile).astype(F32)
            parts.append(_dot(jnp.where(rel == lane, 1.0, 0.0).astype(BF16), y_refs[ee * 2 + kk][...]))
    total = functools.reduce(jnp.add, parts)

    @pl.when(eg == 0)
    def _():
        o_ref[...] = x_ref[...] + total

    @pl.when(eg != 0)
    def _():
        o_ref[...] += total


def moe_combine(x, y, plan, n_experts):
    t, d = x.shape
    tile = MOE_TILE
    group = MOE_COMBINE_GROUP
    n_blocks = y.shape[0] // tile

    def y_spec(ee, kk):
        def y_map(tt, eg, b0, nb):
            idx = tt * n_experts + eg * group + ee
            return (jnp.minimum(b0[idx] + jnp.minimum(kk, jnp.maximum(nb[idx], 1) - 1), n_blocks - 1), 0)
        return pl.BlockSpec((tile, d), y_map)

    grid_spec = pltpu.PrefetchScalarGridSpec(
        num_scalar_prefetch=2,
        grid=(t // tile, n_experts // group),
        in_specs=[pl.BlockSpec((tile, LANES), lambda tt, eg, b0, nb: (tt, 0)),
                  pl.BlockSpec((tile, d), lambda tt, eg, b0, nb: (tt, 0))]
        + [y_spec(ee, kk) for ee in range(group) for kk in range(2)],
        out_specs=pl.BlockSpec((tile, d), lambda tt, eg, b0, nb: (tt, 0)),
    )
    return pl.pallas_call(
        functools.partial(_moe_combine_kernel, n_experts=n_experts, group=group),
        grid_spec=grid_spec,
        out_shape=jax.ShapeDtypeStruct((t, d), F32),
        compiler_params=_cparams("parallel", "arbitrary"),
        name="moe_combine",
    )(plan["b0"], plan["nb"], plan["dest"], x, *([y] * (2 * group)))


def _head_matmul_kernel(a_ref, w_ref, o_ref):
    o_ref[...] = _dot(a_ref[...].astype(BF16), w_ref[...]).astype(o_ref.dtype)


def head_matmul(a, w, *, tm, out_dtype=BF16):
    m = a.shape[0]
    h, k, n = w.shape
    return pl.pallas_call(
        _head_matmul_kernel,
        grid=(m // tm, h),
        in_specs=[pl.BlockSpec((tm, k), lambda i, j: (i, j)),
                  pl.BlockSpec((None, k, n), lambda i, j: (j, 0, 0))],
        out_specs=pl.BlockSpec((tm, n), lambda i, j: (i, j)),
        out_shape=jax.ShapeDtypeStruct((m, h * n), out_dtype),
        compiler_params=_cparams("parallel", "arbitrary"),
        name="head_matmul",
    )(a, w)


def _rope_kernel(x1_ref, x2_ref, cos_ref, sin_ref, r1_ref, r2_ref, *, reps):
    cos, sin = cos_ref[...], sin_ref[...]
    if reps > 1:
        cos = jnp.concatenate([cos] * reps, axis=-1)
        sin = jnp.concatenate([sin] * reps, axis=-1)
    x1, x2 = x1_ref[...], x2_ref[...]
    r1_ref[...] = (x1 * cos - x2 * sin).astype(r1_ref.dtype)
    r2_ref[...] = (x1 * sin + x2 * cos).astype(r2_ref.dtype)


def rope_pairs(x, cos, sin, *, col1, col2, width, tm):
    m = x.shape[0]
    return pl.pallas_call(
        functools.partial(_rope_kernel, reps=width // LANES),
        grid=(m // tm,),
        in_specs=[pl.BlockSpec((tm, width), lambda i: (i, col1)),
                  pl.BlockSpec((tm, width), lambda i: (i, col2)),
                  pl.BlockSpec((tm, LANES), lambda i: (i, 0)),
                  pl.BlockSpec((tm, LANES), lambda i: (i, 0))],
        out_specs=[pl.BlockSpec((tm, width), lambda i: (i, 0))] * 2,
        out_shape=[jax.ShapeDtypeStruct((m, width), F32)] * 2,
        compiler_params=_cparams("parallel"),
        name="rope_pairs",
    )(x, x, cos, sin)


def _cumsum_rows(x, c):
    row = lax.broadcasted_iota(jnp.int32, x.shape, 0)
    sh = 1
    while sh < c:
        x = x + jnp.where(row >= sh, pltpu.roll(x, sh, 0), 0.0)
        sh *= 2
    return x


def _level_masks(c):
    t = lax.broadcasted_iota(jnp.int32, (c, c), 0)
    s = lax.broadcasted_iota(jnp.int32, (c, c), 1)
    masks = []
    blk = 2 * SUBLANES
    while blk <= c:
        half = blk // 2
        masks.append((t // blk == s // blk) & (t % blk >= half) & (s % blk < half))
        blk *= 2
    return masks


def _gla_chunk(q, k, v, lf, s_prev, c, masks):
    dk = q.shape[1]
    b = _cumsum_rows(lf, c)
    rid = lax.broadcasted_iota(jnp.int32, (c, 1), 0) % SUBLANES
    o = jnp.sum(q * k, axis=-1, keepdims=True) * v
    for d in range(1, SUBLANES):
        kd, bd, vd = pltpu.roll(k, d, 0), pltpu.roll(b, d, 0), pltpu.roll(v, d, 0)
        w = jnp.sum(q * kd * jnp.exp(jnp.minimum(b - bd, 0.0)), axis=-1, keepdims=True)
        o = o + jnp.where(rid >= d, w, 0.0) * vd
    if c > SUBLANES:
        a = jnp.zeros((c, c), F32)
        blk = 2 * SUBLANES
        for pair in masks:
            half = blk // 2
            ref = jnp.concatenate(
                [jnp.broadcast_to(b[i * blk + half - 1:i * blk + half, :], (blk, dk)) for i in range(c // blk)], axis=0)
            qt = (q * jnp.exp(jnp.minimum(b - ref, 0.0))).astype(BF16)
            kt = (k * jnp.exp(jnp.minimum(ref - b, 0.0))).astype(BF16)
            a = a + jnp.where(pair, _dot_nt(qt, kt), 0.0)
            blk *= 2
        o = o + _dot(a.astype(BF16), v.astype(BF16))
    o = o + _dot((q * jnp.exp(b)).astype(BF16), s_prev.astype(BF16))
    b_end = b[c - 1:c, :]
    kt = (k * jnp.exp(b_end - b)).astype(BF16)
    decay = jnp.exp(jnp.sum(lf.T, axis=-1, keepdims=True))
    s_new = decay * s_prev + _dot_tn(kt, v.astype(BF16))
    return o, s_new


def _lin_attn_kernel(*refs, mode, sample, c, n_inner, scale):
    refs = list(refs)
    if mode == "hgrn":
        q_ref, f_ref, v_ref, g_ref, lb_ref, gn_ref = refs[:6]
        refs = refs[6:]
    else:
        q_ref, k_ref, v_ref, g_ref, gl_ref, wg_ref, bg_ref, gn_ref = refs[:8]
        refs = refs[8:]
    if sample:
        s0_ref, o_ref, so_ref = refs
    else:
        o_ref, so_ref, s_scr = refs

    def prep(rows):
        if mode == "hgrn":
            qr, fr, lb = q_ref[rows, :], f_ref[rows, :], lb_ref[...]
            q = qr * jax.nn.sigmoid(qr) * scale
            lf = jnp.log(lb + (1.0 - lb) * jax.nn.sigmoid(fr))
            k = (1.0 - lb) * jax.nn.sigmoid(-fr)
        else:
            q = q_ref[rows, :] * scale
            k = k_ref[rows, :]
            z = _dot(gl_ref[rows, :].astype(BF16), wg_ref[...]) + bg_ref[...]
            lf = -(jnp.maximum(-z, 0.0) + jnp.log1p(jnp.exp(-jnp.abs(z)))) * (1.0 / GLA_GATE_NORM)
        return q, k, v_ref[rows, :], lf

    def finish(rows, o):
        g = g_ref[rows, :]
        o_ref[rows, :] = (_norm_rows(o, gn_ref[...]) * (g * jax.nn.sigmoid(g))).astype(o_ref.dtype)

    masks = _level_masks(c)

    if sample:
        def body(si, carry):
            rows = pl.ds(pl.multiple_of(si * c, c), c)
            q, k, v, lf = prep(rows)
            o, s_new = _gla_chunk(q, k, v, lf, s0_ref[si], c, masks)
            so_ref[si] = s_new
            finish(rows, o)
            return carry
    else:
        @pl.when(pl.program_id(2) == 0)
        def _():
            s_scr[...] = jnp.zeros_like(s_scr)

        def body(ci, carry):
            rows = pl.ds(pl.multiple_of(ci * c, c), c)
            q, k, v, lf = prep(rows)
            o, s_new = _gla_chunk(q, k, v, lf, s_scr[...], c, masks)
            s_scr[...] = s_new
            finish(rows, o)
            return carry

    lax.fori_loop(0, n_inner, body, 0, unroll=4)

    if not sample:
        @pl.when(pl.program_id(2) == pl.num_programs(2) - 1)
        def _():
            so_ref[...] = s_scr[...]


def lin_attn(p, extra, gnorm, s0, *, mode, heads, dk, dv, n_seq, seq_len, row0, scale):
    sample = s0 is not None
    if sample:
        c = seq_len
        nb = max(1, min(n_seq, STATE_BLOCK_BYTES // (dk * dv * 4)))
        tb, n_inner = nb * c, nb
        grid = (n_seq // nb, heads, 1)
        rb = row0 // tb
        row = lambda b, h, i: rb + b
    else:
        c, tb = LA_CHUNK, 4 * LA_CHUNK
        n_inner = tb // c
        nt = seq_len // tb
        grid = (n_seq, heads, nt)
        rb = row0 // tb
        row = lambda b, h, i: rb + b * nt + i
    if mode == "hgrn":
        lb, = extra
        nh = heads
        in_specs = [pl.BlockSpec((tb, dk), lambda b, h, i: (row(b, h, i), h)),
                    pl.BlockSpec((tb, dk), lambda b, h, i: (row(b, h, i), nh + h)),
                    pl.BlockSpec((tb, dv), lambda b, h, i: (row(b, h, i), 2 * nh + h)),
                    pl.BlockSpec((tb, dv), lambda b, h, i: (row(b, h, i), 3 * nh + h)),
                    pl.BlockSpec((1, dk), lambda b, h, i: (0, h)),
                    pl.BlockSpec((1, dv), lambda b, h, i: (0, 0))]
        args = [p, p, p, p, lb.reshape(1, -1).astype(F32), gnorm.reshape(1, dv).astype(F32)]
    else:
        wg, bg = extra
        nh = heads
        voff = 2 * heads * dk // dv
        gloff = (2 * heads * dk + 2 * heads * dv) // LANES
        in_specs = [pl.BlockSpec((tb, dk), lambda b, h, i: (row(b, h, i), h)),
                    pl.BlockSpec((tb, dk), lambda b, h, i: (row(b, h, i), nh + h)),
                    pl.BlockSpec((tb, dv), lambda b, h, i: (row(b, h, i), voff + h)),
                    pl.BlockSpec((tb, dv), lambda b, h, i: (row(b, h, i), voff + nh + h)),
                    pl.BlockSpec((tb, LANES), lambda b, h, i: (row(b, h, i), gloff)),
                    pl.BlockSpec((LANES, dk), lambda b, h, i: (0, h)),
                    pl.BlockSpec((1, dk), lambda b, h, i: (0, h)),
                    pl.BlockSpec((1, dv), lambda b, h, i: (0, 0))]
        args = [p, p, p, p, p, wg, bg.reshape(1, -1).astype(F32), gnorm.reshape(1, dv).astype(F32)]
    out_specs = [pl.BlockSpec((tb, dv), lambda b, h, i: (row(b, h, i) - rb, h))]
    out_shape = [jax.ShapeDtypeStruct((n_seq * seq_len, heads * dv), BF16)]
    scratch = []
    if sample:
        in_specs.append(pl.BlockSpec((nb, None, dk, dv), lambda b, h, i: (b, h, 0, 0)))
        args.append(s0)
        out_specs.append(pl.BlockSpec((nb, None, dk, dv), lambda b, h, i: (b, h, 0, 0)))
    else:
        out_specs.append(pl.BlockSpec((None, None, dk, dv), lambda b, h, i: (b, h, 0, 0)))
        scratch.append(pltpu.VMEM((dk, dv), F32))
    out_shape.append(jax.ShapeDtypeStruct((n_seq, heads, dk, dv), F32))
    return pl.pallas_call(
        functools.partial(_lin_attn_kernel, mode=mode, sample=sample, c=c, n_inner=n_inner, scale=scale),
        grid=grid,
        in_specs=in_specs,
        out_specs=out_specs,
        out_shape=out_shape,
        scratch_shapes=scratch,
        compiler_params=_cparams("parallel", "parallel", "arbitrary"),
        name=f"lin_attn_{mode}_{'sample' if sample else 'prompt'}",
    )(*args)


def _moba_weights(gate, m_all, n_prev, own, n_static):
    ridx = lax.broadcasted_iota(jnp.int32, gate.shape, 0)
    cnt = jnp.zeros(gate.shape, F32)
    for jp in range(n_static):
        row = gate[jp:jp + 1, :]
        beats = (row > gate) | ((row == gate) & (jp < ridx))
        cnt = cnt + jnp.where(beats & (jp < n_prev), 1.0, 0.0)
    sel = ((cnt < MB_TOPK) & (ridx < n_prev)) | (ridx == own)
    mv = jnp.where(sel, m_all, NEG_INF)
    return jnp.where(sel, jnp.exp(mv - jnp.max(mv, axis=0, keepdims=True)), 0.0)


def _init_partials(m_scr, l_scr, g_scr):
    m_scr[...] = jnp.full(m_scr.shape, NEG_INF, F32)
    g_scr[...] = jnp.full(g_scr.shape, NEG_INF, F32)
    l_scr[...] = jnp.zeros(l_scr.shape, F32)


def _moba_prompt_kernel(q_ref, k_ref, v_ref, o_ref, qt_scr, m_scr, l_scr, g_scr, w_scr, o_scr, *, scale, group, n_blk):
    i = pl.program_id(2)
    dh = k_ref.shape[1]
    for hh in range(group):
        qt_scr[hh] = (q_ref[:, hh * dh:(hh + 1) * dh] * scale).T.astype(BF16)
    _init_partials(m_scr, l_scr, g_scr)
    kidx = lax.broadcasted_iota(jnp.int32, (MB_BLOCK, MB_BLOCK), 0)
    qidx = lax.broadcasted_iota(jnp.int32, (MB_BLOCK, MB_BLOCK), 1)
    causal = kidx <= qidx

    def blk(j, carry):
        rows = pl.ds(pl.multiple_of(j * MB_BLOCK, MB_BLOCK), MB_BLOCK)
        kj, vj = k_ref[rows, :].astype(BF16), v_ref[rows, :].astype(BF16)
        keep = causal | (j < i)
        for hh in range(group):
            s = _dot(kj, qt_scr[hh])
            g_scr[hh, pl.ds(j, 1), :] = jnp.sum(s, axis=0, keepdims=True)
            s = jnp.where(keep, s, NEG_INF)
            m = jnp.max(s, axis=0, keepdims=True)
            p = jnp.exp(s - m)
            m_scr[hh, pl.ds(j, 1), :] = m
            l_scr[hh, pl.ds(j, 1), :] = jnp.sum(p, axis=0, keepdims=True)
            o_scr[hh, j] = _dot_tn(vj, p.astype(BF16))
        return carry

    lax.fori_loop(0, i + 1, blk, 0)
    for hh in range(group):
        w = _moba_weights(g_scr[hh], m_scr[hh], i, i, n_blk)
        den = jnp.sum(w * l_scr[hh], axis=0, keepdims=True)
        w_scr[...] = w

        def comb(j, acc):
            return acc + w_scr[pl.ds(j, 1), :] * o_scr[hh, j]

        acc = lax.fori_loop(0, i + 1, comb, jnp.zeros((dh, MB_BLOCK), F32))
        o_ref[:, hh * dh:(hh + 1) * dh] = (acc / den).T.astype(o_ref.dtype)


def moba_prompt(p, *, n_seq, seq_len, heads, kv_heads, dh):
    group = heads // kv_heads
    nq = seq_len // MB_BLOCK
    nbp = _round_up(nq, SUBLANES)
    koff = heads
    voff = heads + kv_heads
    return pl.pallas_call(
        functools.partial(_moba_prompt_kernel, scale=dh ** -0.5, group=group, n_blk=nq),
        grid=(n_seq, kv_heads, nq),
        in_specs=[pl.BlockSpec((MB_BLOCK, group * dh), lambda b, g, i: (b * nq + i, g)),
                  pl.BlockSpec((seq_len, dh), lambda b, g, i: (b, koff + g)),
                  pl.BlockSpec((seq_len, dh), lambda b, g, i: (b, voff + g))],
        out_specs=pl.BlockSpec((MB_BLOCK, group * dh), lambda b, g, i: (b * nq + i, g)),
        out_shape=jax.ShapeDtypeStruct((n_seq * seq_len, heads * dh), BF16),
        scratch_shapes=[pltpu.VMEM((group, dh, MB_BLOCK), BF16)]
        + [pltpu.VMEM((group, nbp, MB_BLOCK), F32)] * 3
        + [pltpu.VMEM((nbp, MB_BLOCK), F32), pltpu.VMEM((group, nq, dh, MB_BLOCK), F32)],
        compiler_params=_cparams("parallel", "parallel", "arbitrary"),
        name="moba_prompt",
    )(p, p, p)


def _moba_sample_kernel(pt_ref, qbd_ref, kn_ref, vn_ref, *refs, pages, kv_heads, n_new):
    del pt_ref
    k_refs, v_refs = refs[:pages], refs[pages:2 * pages]
    o_ref, m_scr, l_scr, g_scr, o_scr = refs[2 * pages:]
    g = pl.program_id(1)
    rows, dh = o_ref.shape
    rpk = rows // kv_heads
    ppb = MB_BLOCK // PAGE_SIZE
    bps = pages // ppb

    def scores(k_ref, n):
        parts = [_dot(k_ref[pl.ds(gk, n, stride=kv_heads), :].astype(BF16), qbd_ref[gk * dh:(gk + 1) * dh, :])
                 for gk in range(kv_heads)]
        return functools.reduce(jnp.add, parts)

    def values(v_ref, p, n):
        lane_g = lax.broadcasted_iota(jnp.int32, p.shape, 1) // rpk
        parts = [_dot_tn(v_ref[pl.ds(gk, n, stride=kv_heads), :].astype(BF16),
                         jnp.where(lane_g == gk, p, 0.0).astype(BF16)) for gk in range(kv_heads)]
        return functools.reduce(jnp.add, parts)

    @pl.when(g == 0)
    def _():
        _init_partials(m_scr, l_scr, g_scr)

    for bb in range(bps):
        j = g * bps + bb
        ss = [scores(k_refs[bb * ppb + t], PAGE_SIZE) for t in range(ppb)]
        g_scr[pl.ds(j, 1), :] = functools.reduce(jnp.add, [jnp.sum(s, axis=0, keepdims=True) for s in ss])
        m = functools.reduce(jnp.maximum, [jnp.max(s, axis=0, keepdims=True) for s in ss])
        ps = [jnp.exp(s - m) for s in ss]
        m_scr[pl.ds(j, 1), :] = m
        l_scr[pl.ds(j, 1), :] = functools.reduce(jnp.add, [jnp.sum(p, axis=0, keepdims=True) for p in ps])
        o_scr[j] = functools.reduce(jnp.add, [values(v_refs[bb * ppb + t], ps[t], PAGE_SIZE) for t in range(ppb)])

    @pl.when(g == pl.num_programs(1) - 1)
    def _():
        n_prev = o_scr.shape[0] - 1
        s = scores(kn_ref, n_new)
        tok = lax.broadcasted_iota(jnp.int32, s.shape, 0)
        qi = lax.broadcasted_iota(jnp.int32, s.shape, 1) % n_new
        s = jnp.where(tok <= qi, s, NEG_INF)
        m = jnp.max(s, axis=0, keepdims=True)
        p = jnp.exp(s - m)
        m_scr[n_prev:n_prev + 1, :] = m
        l_scr[n_prev:n_prev + 1, :] = jnp.sum(p, axis=0, keepdims=True)
        o_scr[n_prev] = values(vn_ref, p, n_new)
        w = _moba_weights(g_scr[...], m_scr[...], n_prev, n_prev, n_prev)
        den = jnp.sum(w * l_scr[...], axis=0, keepdims=True)
        acc = jnp.zeros((dh, rows), F32)
        for jj in range(n_prev + 1):
            acc = acc + w[jj:jj + 1, :] * o_scr[jj]
        o_ref[...] = (acc / den).T.astype(o_ref.dtype)


def moba_sample(qbd, k_new, v_new, cache_k, cache_v, page_table, layer, *, kv_heads, dh):
    n_seq, _, rows = qbd.shape
    n_new = k_new.shape[1] // kv_heads
    n_pages = page_table.shape[1]
    pages = PAGES_PER_STEP
    n_prev = n_pages * PAGE_SIZE // MB_BLOCK
    nbp = _round_up(n_prev + 1, SUBLANES)
    assert n_pages % pages == 0 and pages % (MB_BLOCK // PAGE_SIZE) == 0
    assert n_new <= MB_BLOCK

    def page_spec(t):
        return pl.BlockSpec((None, None, PAGE_SIZE * kv_heads, dh),
                            lambda b, g, pt: (layer, pt[b, g * pages + t], 0, 0))

    grid_spec = pltpu.PrefetchScalarGridSpec(
        num_scalar_prefetch=1,
        grid=(n_seq, n_pages // pages),
        in_specs=[pl.BlockSpec((None, kv_heads * dh, rows), lambda b, g, pt: (b, 0, 0)),
                  pl.BlockSpec((None, n_new * kv_heads, dh), lambda b, g, pt: (b, 0, 0)),
                  pl.BlockSpec((None, n_new * kv_heads, dh), lambda b, g, pt: (b, 0, 0))]
        + [page_spec(t) for t in range(pages)] * 2,
        out_specs=pl.BlockSpec((None, rows, dh), lambda b, g, pt: (b, 0, 0)),
        scratch_shapes=[pltpu.VMEM((nbp, rows), F32)] * 3 + [pltpu.VMEM((n_prev + 1, dh, rows), F32)],
    )
    return pl.pallas_call(
        functools.partial(_moba_sample_kernel, pages=pages, kv_heads=kv_heads, n_new=n_new),
        grid_spec=grid_spec,
        out_shape=jax.ShapeDtypeStruct((n_seq, rows, dh), F32),
        compiler_params=_cparams("parallel", "arbitrary"),
        name="moba_sample",
    )(page_table, qbd, k_new, v_new, *([cache_k] * pages), *([cache_v] * pages))


def _flash_init(m_scr, l_scr, acc_scr):
    m_scr[...] = jnp.full(m_scr.shape, NEG_INF, F32)
    l_scr[...] = jnp.zeros(l_scr.shape, F32)
    acc_scr[...] = jnp.zeros(acc_scr.shape, F32)


def _flash_update(s_list, pv, m_scr, l_scr, acc_scr):
    m_old = m_scr[...]
    m_new = functools.reduce(jnp.maximum, [jnp.max(s, axis=0, keepdims=True) for s in s_list] + [m_old])
    alpha = jnp.exp(m_old - m_new)
    ps = [jnp.exp(s - m_new) for s in s_list]
    l_scr[...] = alpha * l_scr[...] + functools.reduce(jnp.add, [jnp.sum(p, axis=0, keepdims=True) for p in ps])
    acc_scr[...] = alpha * acc_scr[...] + functools.reduce(
        jnp.add, [pv(idx, p.astype(BF16)) for idx, p in enumerate(ps)])
    m_scr[...] = m_new


def _mla_prompt_kernel(q_ref, qrt_ref, c_ref, ct_ref, kr_ref, wuk_ref, wuv_ref, o_ref, qlt_scr, m_scr, l_scr, acc_scr,
                       *, tq, tk, heads):
    i = pl.program_id(1)
    nope, vdim = wuk_ref.shape[2], wuv_ref.shape[2]
    for h in range(heads):
        qh = q_ref[:, h * nope:(h + 1) * nope].astype(BF16)
        qlt_scr[:, h * tq:(h + 1) * tq] = _dot_nt(wuk_ref[h], qh).astype(BF16)
    gw = m_scr.shape[1]
    hpg = gw // tq
    kidx = lax.broadcasted_iota(jnp.int32, (tk, gw), 0)
    qpos = i * tq + lax.broadcasted_iota(jnp.int32, (tk, gw), 1) % tq
    n_kb = ((i + 1) * tq + tk - 1) // tk
    for cg in range(heads // hpg):
        lanes = slice(cg * gw, (cg + 1) * gw)
        _flash_init(m_scr, l_scr, acc_scr)

        def body(j, carry):
            r0 = pl.multiple_of(j * tk, tk)
            s = _dot(c_ref[pl.ds(r0, tk), :], qlt_scr[:, lanes]) + _dot(kr_ref[pl.ds(r0, tk), :], qrt_ref[:, lanes])
            s = jnp.where(r0 + kidx <= qpos, s, NEG_INF)
            _flash_update([s], lambda idx, p: _dot(ct_ref[j], p), m_scr, l_scr, acc_scr)
            return carry

        lax.fori_loop(0, n_kb, body, 0)
        ot = (acc_scr[...] / l_scr[...]).astype(BF16)
        for hh in range(hpg):
            h = cg * hpg + hh
            o_ref[:, h * vdim:(h + 1) * vdim] = _dot_tn(ot[:, hh * tq:(hh + 1) * tq], wuv_ref[h]).astype(o_ref.dtype)


def mla_prompt(q, qrt, c, ct, kr, wuk, wuv, *, n_seq, seq_len):
    tq, tk = MLA_TQ, MLA_TK
    nq = seq_len // tq
    heads, cdim, nope = wuk.shape
    vdim = wuv.shape[2]
    rdim = kr.shape[1]
    return pl.pallas_call(
        functools.partial(_mla_prompt_kernel, tq=tq, tk=tk, heads=heads),
        grid=(n_seq, nq),
        in_specs=[pl.BlockSpec((tq, heads * nope), lambda b, i: (b * nq + i, 0)),
                  pl.BlockSpec((None, rdim, heads * tq), lambda b, i: (b * nq + i, 0, 0)),
                  pl.BlockSpec((seq_len, cdim), lambda b, i: (b, 0)),
                  pl.BlockSpec((None, seq_len // tk, cdim, tk), lambda b, i: (b, 0, 0, 0)),
                  pl.BlockSpec((seq_len, rdim), lambda b, i: (b, 0)),
                  pl.BlockSpec((heads, cdim, nope), lambda b, i: (0, 0, 0)),
                  pl.BlockSpec((heads, cdim, vdim), lambda b, i: (0, 0, 0))],
        out_specs=pl.BlockSpec((tq, heads * vdim), lambda b, i: (b * nq + i, 0)),
        out_shape=jax.ShapeDtypeStruct((n_seq * seq_len, heads * vdim), BF16),
        scratch_shapes=[pltpu.VMEM((cdim, heads * tq), BF16), pltpu.VMEM((1, MLA_GROUP_LANES), F32),
                        pltpu.VMEM((1, MLA_GROUP_LANES), F32), pltpu.VMEM((cdim, MLA_GROUP_LANES), F32)],
        compiler_params=_cparams("parallel", "arbitrary"),
        name="mla_prompt",
    )(q, qrt, c, ct, kr, wuk, wuv)


def _mla_sample_kernel(pt_ref, qlt_ref, qrt_ref, cn_ref, krn_ref, *refs, pages, heads):
    del pt_ref
    c_refs, krt_refs = refs[:pages], refs[pages:2 * pages]
    o_ref, m_scr, l_scr, acc_scr, call_scr, krall_scr = refs[2 * pages:]
    g = pl.program_id(1)

    @pl.when(g == 0)
    def _():
        _flash_init(m_scr, l_scr, acc_scr)

    qlt, qrt = qlt_ref[...], qrt_ref[...]
    n_chains = m_scr.shape[0]
    per = pages // n_chains
    for ch in range(n_chains):
        for t in range(per):
            call_scr[ch, t * PAGE_SIZE:(t + 1) * PAGE_SIZE, :] = c_refs[ch * per + t][...].astype(BF16)
            krall_scr[ch, :, t * PAGE_SIZE:(t + 1) * PAGE_SIZE] = krt_refs[ch * per + t][...].astype(BF16)
        c_all = call_scr[ch]
        s = _dot(c_all, qlt) + _dot_tn(krall_scr[ch], qrt)
        _flash_update([s], lambda idx, p, c_all=c_all: _dot_tn(c_all, p), m_scr.at[ch], l_scr.at[ch], acc_scr.at[ch])

    @pl.when(g == pl.num_programs(1) - 1)
    def _():
        cn = cn_ref[...].astype(BF16)
        s = _dot(cn, qlt) + _dot(krn_ref[...].astype(BF16), qrt)
        tok = lax.broadcasted_iota(jnp.int32, s.shape, 0)
        qi = lax.broadcasted_iota(jnp.int32, s.shape, 1) // heads
        s = jnp.where(tok <= qi, s, NEG_INF)
        _flash_update([s], lambda idx, p: _dot_tn(cn, p), m_scr.at[0], l_scr.at[0], acc_scr.at[0])
        m_all = functools.reduce(jnp.maximum, [m_scr[ch] for ch in range(n_chains)])
        ws = [jnp.exp(m_scr[ch] - m_all) for ch in range(n_chains)]
        l_all = functools.reduce(jnp.add, [ws[ch] * l_scr[ch] for ch in range(n_chains)])
        acc = functools.reduce(jnp.add, [ws[ch] * acc_scr[ch] for ch in range(n_chains)])
        o_ref[...] = (acc / l_all).T.astype(o_ref.dtype)


def mla_sample(qlt, qrt, c_new, kr_new, cache_c, cache_krt, page_table, layer, *, heads):
    n_seq, n_pages = page_table.shape
    pages = PAGES_PER_STEP
    n_new = c_new.shape[0] // n_seq
    rows = n_new * heads
    cdim, rdim = c_new.shape[1], kr_new.shape[1]
    assert n_pages % pages == 0

    def c_spec(t):
        return pl.BlockSpec((None, None, PAGE_SIZE, cdim), lambda b, g, pt: (layer, pt[b, g * pages + t], 0, 0))

    def kr_spec(t):
        return pl.BlockSpec((None, None, rdim, PAGE_SIZE), lambda b, g, pt: (layer, pt[b, g * pages + t], 0, 0))

    grid_spec = pltpu.PrefetchScalarGridSpec(
        num_scalar_prefetch=1,
        grid=(n_seq, n_pages // pages),
        in_specs=[pl.BlockSpec((None, cdim, rows), lambda b, g, pt: (b, 0, 0)),
                  pl.BlockSpec((None, rdim, rows), lambda b, g, pt: (b, 0, 0)),
                  pl.BlockSpec((n_new, cdim), lambda b, g, pt: (b, 0)),
                  pl.BlockSpec((n_new, rdim), lambda b, g, pt: (b, 0))]
        + [c_spec(t) for t in range(pages)] + [kr_spec(t) for t in range(pages)],
        out_specs=pl.BlockSpec((rows, cdim), lambda b, g, pt: (b, 0)),
        scratch_shapes=[pltpu.VMEM((MLA_DECODE_CHAINS, 1, rows), F32), pltpu.VMEM((MLA_DECODE_CHAINS, 1, rows), F32),
                        pltpu.VMEM((MLA_DECODE_CHAINS, cdim, rows), F32),
                        pltpu.VMEM((MLA_DECODE_CHAINS, pages // MLA_DECODE_CHAINS * PAGE_SIZE, cdim), BF16),
                        pltpu.VMEM((MLA_DECODE_CHAINS, rdim, pages // MLA_DECODE_CHAINS * PAGE_SIZE), BF16)],
    )
    return pl.pallas_call(
        functools.partial(_mla_sample_kernel, pages=pages, heads=heads),
        grid_spec=grid_spec,
        out_shape=jax.ShapeDtypeStruct((n_seq * rows, cdim), BF16),
        compiler_params=_cparams("parallel", "arbitrary"),
        name="mla_sample",
    )(page_table, qlt, qrt, c_new, kr_new, *([cache_c] * pages), *([cache_krt] * pages))


TM = 1024
MIXER_OUT_TK = 2048
FFN_OUT_K_SPLITS = 4


def _hgrn_layer(x, norm_g, w_in, w_out, layer, lb, gnorm, s0, *, n_prompt, prompt_len, n_sample, sample_len):
    d = x.shape[1]
    dk = d // HG_HEADS
    p = norm_matmul(x, norm_g, w_in, layer, tm=TM, tn=512)
    kw = dict(mode="hgrn", heads=HG_HEADS, dk=dk, dv=dk, scale=dk ** -0.5)
    op, sp = lin_attn(p, (lb,), gnorm, None, n_seq=n_prompt, seq_len=prompt_len, row0=0, **kw)
    os_, ss = lin_attn(p, (lb,), gnorm, s0, n_seq=n_sample, seq_len=sample_len, row0=n_prompt * prompt_len, **kw)
    o = jnp.concatenate([op, os_], axis=0)
    return matmul_res(o, w_out, layer, x, tm=TM, tn=1024, tk=MIXER_OUT_TK), sp, ss


def _gla_layer(x, norm_g, w_in, w_gate_up, b_gate, gnorm, w_out, layer, s0, *, n_prompt, prompt_len, n_sample,
               sample_len):
    d = x.shape[1]
    dk, dv = d // 2 // GLA_HEADS, d // GLA_HEADS
    w_pad = jnp.pad(w_in, ((0, 0), (0, LANES - GLA_GATE_RANK))).astype(BF16)[None]
    p = norm_matmul(x, norm_g, w_pad, 0, tm=TM, tn=w_pad.shape[2] // 7)
    wg = jnp.pad(w_gate_up, ((0, LANES - GLA_GATE_RANK), (0, 0))).astype(BF16)
    kw = dict(mode="gla", heads=GLA_HEADS, dk=dk, dv=dv, scale=dk ** -0.5)
    op, sp = lin_attn(p, (wg, b_gate), gnorm, None, n_seq=n_prompt, seq_len=prompt_len, row0=0, **kw)
    os_, ss = lin_attn(p, (wg, b_gate), gnorm, s0, n_seq=n_sample, seq_len=sample_len,
                       row0=n_prompt * prompt_len, **kw)
    o = jnp.concatenate([op, os_], axis=0)
    return matmul_res(o, w_out, layer, x, tm=TM, tn=1024, tk=MIXER_OUT_TK), sp, ss


def _moba_layer(x, norm_g, w_in, w_out, layer, cache_k, cache_v, page_table, *, n_prompt, prompt_len, n_sample,
                sample_len):
    dh, h, hkv = MB_HEAD_DIM, MB_HEADS, MB_KV_HEADS
    tp = n_prompt * prompt_len
    p = norm_matmul(x, norm_g, w_in, layer, tm=TM, tn=512)
    k = p[:, h * dh:(h + hkv) * dh]
    v = p[:, (h + hkv) * dh:]
    ap = moba_prompt(p, n_seq=n_prompt, seq_len=prompt_len, heads=h, kv_heads=hkv, dh=dh)
    qt = (p[tp:, :h * dh] * dh ** -0.5).reshape(n_sample, sample_len, h, dh).transpose(0, 3, 2, 1)
    qt = qt.reshape(n_sample, dh, h * sample_len)
    kv_of_col = jnp.arange(h * sample_len) // (h // hkv * sample_len)
    qbd = jnp.where(kv_of_col[None, None, None, :] == jnp.arange(hkv)[None, :, None, None], qt[:, None], 0.0)
    qbd = qbd.reshape(n_sample, hkv * dh, h * sample_len).astype(BF16)
    ks = k[tp:].reshape(n_sample, sample_len * hkv, dh)
    vs = v[tp:].reshape(n_sample, sample_len * hkv, dh)
    n_layers, n_pool = cache_k.shape[:2]
    ck = cache_k.reshape(n_layers, n_pool, PAGE_SIZE * hkv, dh)
    cv = cache_v.reshape(n_layers, n_pool, PAGE_SIZE * hkv, dh)
    as_ = moba_sample(qbd, ks, vs, ck, cv, page_table, layer, kv_heads=hkv, dh=dh)
    as_ = as_.reshape(n_sample, h, sample_len, dh).transpose(0, 2, 1, 3).reshape(n_sample * sample_len, h * dh)
    a = jnp.concatenate([ap, as_.astype(BF16)], axis=0)
    x = matmul_res(a, w_out, layer, x, tm=TM, tn=1024, tk=MIXER_OUT_TK)
    shape_p = (n_prompt, prompt_len, hkv, dh)
    shape_s = (n_sample, sample_len, hkv, dh)
    return x, k[:tp].reshape(shape_p), v[:tp].reshape(shape_p), k[tp:].reshape(shape_s), v[tp:].reshape(shape_s)


def _rope_tables(pos):
    half = MLA_ROPE // 2
    inv = ROPE_THETA ** (-jnp.arange(half, dtype=F32) / half)
    ang = pos.astype(F32)[:, None] * inv[None, :]
    reps = LANES // half
    return jnp.tile(jnp.cos(ang), (1, reps)), jnp.tile(jnp.sin(ang), (1, reps))


def _mla_layer(x, norm_g, w_in, q_norm, w_q_up, kv_norm, w_kv_up, w_out, layer, cache_c, cache_kr, page_table, pos, *,
               n_prompt, prompt_len, n_sample, sample_len):
    t = x.shape[0]
    h, half = MLA_HEADS, MLA_ROPE // 2
    tp = n_prompt * prompt_len
    d = w_in.shape[0]
    scale = (MLA_NOPE + MLA_ROPE) ** -0.5
    qa_w, ckv_w, kr_w = w_in[:, :MLA_Q_RANK], w_in[:, MLA_Q_RANK:MLA_Q_RANK + MLA_KV_RANK], w_in[:, MLA_Q_RANK + MLA_KV_RANK:]
    qpad = 1024 - MLA_Q_RANK
    zeros = lambda n: jnp.zeros((d, n), w_in.dtype)
    w_p = jnp.concatenate([qa_w, zeros(qpad), ckv_w, kr_w[:, :half], zeros(LANES - half), kr_w[:, half:],
                           zeros(LANES - half)], axis=1).astype(BF16)[None]
    p = norm_matmul(x, norm_g, w_p, 0, tm=TM, tn=256)
    w4 = w_q_up.reshape(MLA_Q_RANK, h, MLA_NOPE + MLA_ROPE)
    w_q = jnp.concatenate([w4[:, :, :MLA_NOPE].reshape(MLA_Q_RANK, h * MLA_NOPE),
                           w4[:, :, MLA_NOPE:MLA_NOPE + half].reshape(MLA_Q_RANK, h * half),
                           w4[:, :, MLA_NOPE + half:].reshape(MLA_Q_RANK, h * half)], axis=1).astype(BF16)[None]
    q3 = norm_matmul(p, q_norm, w_q, 0, tm=TM, tn=512)
    c32 = rmsnorm_rows(p, kv_norm, col=1024 // MLA_KV_RANK, tm=TM)
    cos, sin = _rope_tables(pos)
    rc = h * MLA_NOPE // (h * half)
    qr1, qr2 = rope_pairs(q3, cos, sin, col1=rc, col2=rc + 1, width=h * half, tm=TM)
    kc = (1024 + MLA_KV_RANK) // LANES
    kr1, kr2 = rope_pairs(p, cos, sin, col1=kc, col2=kc + 1, width=LANES, tm=TM)
    kr = jnp.concatenate([kr1[:, :half], kr2[:, :half]], axis=1)
    q_rope = (jnp.concatenate([qr1.reshape(t, h, half), qr2.reshape(t, h, half)], axis=-1) * scale).astype(BF16)
    wkv = w_kv_up.reshape(MLA_KV_RANK, h, MLA_NOPE + MLA_V)
    w_uk = wkv[:, :, :MLA_NOPE] * scale
    w_uv = wkv[:, :, MLA_NOPE:].transpose(1, 0, 2).astype(BF16)
    c16, kr16 = c32.astype(BF16), kr.astype(BF16)
    tq, tk = MLA_TQ, MLA_TK
    qrt_p = q_rope[:tp].reshape(tp // tq, tq, h, MLA_ROPE).transpose(0, 3, 2, 1).reshape(tp // tq, MLA_ROPE, h * tq)
    ct = c16[:tp].reshape(n_prompt, prompt_len // tk, tk, MLA_KV_RANK).transpose(0, 1, 3, 2)
    op = mla_prompt(q3, qrt_p, c16, ct, kr16, w_uk.transpose(1, 0, 2).astype(BF16), w_uv,
                    n_seq=n_prompt, seq_len=prompt_len)
    ql_s = head_matmul(q3[tp:], w_uk.transpose(1, 2, 0).astype(BF16), tm=n_sample * sample_len)
    rows = sample_len * h
    qlt_s = ql_s.reshape(n_sample, rows, MLA_KV_RANK).transpose(0, 2, 1)
    qrt_s = q_rope[tp:].reshape(n_sample, rows, MLA_ROPE).transpose(0, 2, 1)
    os_ = mla_sample(qlt_s, qrt_s, c32[tp:], kr[tp:], cache_c, jnp.swapaxes(cache_kr, 2, 3), page_table, layer, heads=h)
    os_ = head_matmul(os_.reshape(n_sample * sample_len, h * MLA_KV_RANK), w_uv, tm=n_sample * sample_len)
    o = jnp.concatenate([op, os_], axis=0)
    x = matmul_res(o, w_out, layer, x, tm=TM, tn=1024, tk=MIXER_OUT_TK)
    return (x, c32[:tp].reshape(n_prompt, prompt_len, -1), kr[:tp].reshape(n_prompt, prompt_len, -1),
            c32[tp:].reshape(n_sample, sample_len, -1), kr[tp:].reshape(n_sample, sample_len, -1))


def _dense_ffn(x, norm_g, w_in, w_out, layer):
    act = swiglu_proj(x, norm_g, w_in[:, None], layer, None, tm=TM, tn=512)
    return matmul_res(act, w_out, layer, x, tm=TM, tn=1024, tk=w_out.shape[1] // FFN_OUT_K_SPLITS)


def _moe_ffn(x, norm_g, wr, w_in, w_out, layer):
    rw, h = router(x, norm_g, wr, tm=512)
    n_experts = wr.shape[1]
    plan = _moe_plan(rw, n_experts)
    xs, row_w = moe_gather(h, plan)
    act = moe_up(xs, row_w, w_in, layer, plan)
    y = moe_down(act, w_out, layer, plan)
    return moe_combine(x, y, plan, n_experts)


def kernel(x_prompt, x_sample, state_hgrn, state_gla, cache_moba_k, cache_moba_v, cache_mla_latent, cache_mla_krope,
           page_table, norm_mixer, norm_ffn, norm_final, hgrn_w_in, hgrn_lb_logits, hgrn_gnorm, hgrn_w_out, gla_w_in,
           gla_w_gate_up, gla_b_gate, gla_gnorm, gla_w_out, moba_w_in, moba_w_out, mla_w_in, mla_q_norm, mla_w_q_up,
           mla_kv_norm, mla_w_kv_up, mla_w_out, ffn_w_in, ffn_w_out, moe_router, moe_w_in, moe_w_out):
    n_prompt, prompt_len, d = x_prompt.shape
    n_sample, sample_len, _ = x_sample.shape
    past_len = page_table.shape[1] * PAGE_SIZE
    depth = norm_mixer.shape[0]
    dims = dict(n_prompt=n_prompt, prompt_len=prompt_len, n_sample=n_sample, sample_len=sample_len)
    tp = n_prompt * prompt_len
    pos = jnp.concatenate([jnp.tile(jnp.arange(prompt_len, dtype=jnp.int32), n_prompt),
                           jnp.tile(past_len + jnp.arange(sample_len, dtype=jnp.int32), n_sample)])
    lb_all = jnp.cumsum(jax.nn.softmax(hgrn_lb_logits.astype(F32), axis=0), axis=0)
    x = jnp.concatenate([x_prompt.reshape(tp, d), x_sample.reshape(-1, d)], axis=0)
    bf = lambda w: w.astype(BF16)
    hgrn_w_in, hgrn_w_out, gla_w_out, moba_w_in, moba_w_out, mla_w_out = (
        bf(hgrn_w_in), bf(hgrn_w_out), bf(gla_w_out), bf(moba_w_in), bf(moba_w_out), bf(mla_w_out))
    ffn_w_in, ffn_w_out, moe_w_in, moe_w_out = bf(ffn_w_in), bf(ffn_w_out), bf(moe_w_in), bf(moe_w_out)
    outs = {k: [] for k in ("hg_p", "hg_s", "gla_p", "gla_s", "mbk_p", "mbv_p", "mbk_s", "mbv_s",
                            "mlc_p", "mlr_p", "mlc_s", "mlr_s")}
    for i in range(depth):
        m, j = i % 4, i // 4
        if m == 0:
            x, sp, ss = _hgrn_layer(x, norm_mixer[i], hgrn_w_in, hgrn_w_out, j, lb_all[i], hgrn_gnorm[j],
                                    state_hgrn[j], **dims)
            outs["hg_p"].append(sp)
            outs["hg_s"].append(ss)
        elif m == 1:
            x, sp, ss = _gla_layer(x, norm_mixer[i], gla_w_in[j], gla_w_gate_up[j], gla_b_gate[j], gla_gnorm[j],
                                   gla_w_out, j, state_gla[j], **dims)
            outs["gla_p"].append(sp)
            outs["gla_s"].append(ss)
        elif m == 2:
            x, kp, vp, ks, vs = _moba_layer(x, norm_mixer[i], moba_w_in, moba_w_out, j, cache_moba_k, cache_moba_v,
                                            page_table, **dims)
            for key, val in zip(("mbk_p", "mbv_p", "mbk_s", "mbv_s"), (kp, vp, ks, vs)):
                outs[key].append(val)
        else:
            x, cp, rp, cs, rs = _mla_layer(x, norm_mixer[i], mla_w_in[j], mla_q_norm[j], mla_w_q_up[j], mla_kv_norm[j],
                                           mla_w_kv_up[j], mla_w_out, j, cache_mla_latent, cache_mla_krope, page_table,
                                           pos, **dims)
            for key, val in zip(("mlc_p", "mlr_p", "mlc_s", "mlr_s"), (cp, rp, cs, rs)):
                outs[key].append(val)
        if i % 2 == 0:
            x = _dense_ffn(x, norm_ffn[i], ffn_w_in, ffn_w_out, i // 2)
        else:
            x = _moe_ffn(x, norm_ffn[i], moe_router[i // 2], moe_w_in, moe_w_out, i // 2)
    y = rmsnorm_rows(x, norm_final, tm=TM)
    return (y[:tp].reshape(x_prompt.shape), y[tp:].reshape(x_sample.shape),
            *(jnp.stack(outs[k]) for k in ("hg_p", "hg_s", "gla_p", "gla_s", "mbk_p", "mbv_p", "mbk_s", "mbv_s",
                                           "mlc_p", "mlr_p", "mlc_s", "mlr_s")))
```
